```python
import math
import jax
import jax.numpy as jnp
from jax import lax
import numpy as np

D_MODEL = 1024
BATCH = 4
SEQ = 8192
DEPTH = 2

N_HEADS_A = 8
HEAD_DIM_A = 64
ROPE_DIM_A = HEAD_DIM_A // 4
DILATED_BRANCHES = ((128, 1), (512, 4), (2048, 16))
MLA_HEADS = 8
MLA_NOPE = 64
MLA_ROPE = 32
MLA_V = 64
MLA_Q_RANK = 256
MLA_KV_RANK = 128
ROPE_THETA = 500000.0
Q_BLOCK = 128
MIX_A = N_HEADS_A * HEAD_DIM_A
MIX_B = MLA_HEADS * MLA_V
ATTN_IN = 3 * MIX_A + MLA_Q_RANK + MLA_KV_RANK + MLA_ROPE
D_INNER = 2 * D_MODEL
SSD_HEADDIM = 64
SSD_HEADS = D_INNER // SSD_HEADDIM
SSD_GROUPS = 4
D_STATE = 128
D_CONV = 4
CHUNK = 256
XBC_DIM = D_INNER + 2 * SSD_GROUPS * D_STATE
SSD_IN = D_INNER + XBC_DIM + SSD_HEADS
N_GROUPS = 4
EXPERTS_PER_GROUP = 8
N_EXPERTS = N_GROUPS * EXPERTS_PER_GROUP
TOP_K_IN_GROUP = 2
D_EXPERT = 512
MOE_BLOCK = 128
NORM_EPS = 1e-6

kernel_name = 'hybrid_dilated_mla_ssd_hmoe'


def rmsnorm(x, g):
    xf = x.astype(jnp.float32)
    y = xf * lax.rsqrt(jnp.mean(xf * xf, axis=-1, keepdims=True) + NORM_EPS)
    return (y * g.astype(jnp.float32)).astype(x.dtype)


def rope_tables(positions, rot_dim):
    inv_freq = ROPE_THETA ** (-jnp.arange(0, rot_dim, 2, dtype=jnp.float32) / rot_dim)
    ang = positions.astype(jnp.float32)[..., None] * inv_freq
    return jnp.cos(ang), jnp.sin(ang)


def apply_rope(x, cos, sin):
    half = x.shape[-1] // 2
    c = cos[:, :, None, :].astype(x.dtype)
    s = sin[:, :, None, :].astype(x.dtype)
    x1, x2 = x[..., :half], x[..., half:]
    return jnp.concatenate([x1 * c - x2 * s, x2 * c + x1 * s], axis=-1)


def partial_rope(x, cos, sin):
    return jnp.concatenate([apply_rope(x[..., :ROPE_DIM_A], cos, sin), x[..., ROPE_DIM_A:]], axis=-1)


def dilated_branch(q, k, v, window, dilation):
    Bsz, S, H, Dh = q.shape
    span = window // dilation
    unit = span * dilation
    Sp = -(-S // unit) * unit
    nb = Sp // unit

    def residue_blocks(t):
        t = jnp.pad(t, ((0, 0), (0, Sp - S), (0, 0), (0, 0)))
        return t.reshape(Bsz, nb, span, dilation, H, Dh)

    def with_prev(t):
        prev = jnp.pad(t, ((0, 0), (1, 0), (0, 0), (0, 0), (0, 0), (0, 0)))[:, :-1]
        return jnp.concatenate([prev, t], axis=2)

    qs = residue_blocks(q)
    ks = with_prev(residue_blocks(k))
    vs = with_prev(residue_blocks(v))
    s = jnp.einsum('bnqrhd,bnkrhd->bnrhqk', qs, ks).astype(jnp.float32) * (Dh ** -0.5)
    qi = jnp.arange(span)[:, None]
    ki = jnp.arange(2 * span)[None, :]
    dist = qi + span - ki
    band = (dist >= 0) & (dist <= span)
    has_prev = (jnp.arange(nb)[:, None, None] > 0) | (ki[None] >= span)
    valid = band[None] & has_prev
    s = jnp.where(valid[None, :, None, None], s, -jnp.inf)
    m = jnp.max(s, axis=-1, keepdims=True)
    p = jnp.exp(s - m)
    l = jnp.sum(p, axis=-1)
    o = jnp.einsum('bnrhqk,bnkrhd->bnqrhd', p.astype(v.dtype), vs).astype(jnp.float32)
    l_q = jnp.moveaxis(l, -1, 2)
    o = o / l_q[..., None]
    lse = jnp.moveaxis(m[..., 0], -1, 2) + jnp.log(l_q)
    return o.reshape(Bsz, Sp, H, Dh)[:, :S], lse.reshape(Bsz, Sp, H)[:, :S]


def dilated_mixture_attention(q, k, v):
    outs, lses = [], []
    for window, dilation in DILATED_BRANCHES:
        o, lse = dilated_branch(q, k, v, window, dilation)
        outs.append(o)
        lses.append(lse)
    alpha = jax.nn.softmax(jnp.stack(lses), axis=0)
    return jnp.sum(alpha[..., None] * jnp.stack(outs), axis=0).astype(q.dtype)


def causal_block_attention(q, k, v):
    Bsz, S, H, Dk = q.shape
    nq = S // Q_BLOCK
    scale = Dk ** -0.5
    q_blocks = q.reshape(Bsz, nq, Q_BLOCK, H, Dk).swapaxes(0, 1)
    k_pos = jnp.arange(S)

    def one_block(args):
        qb, start = args
        s = jnp.einsum('bqhd,bkhd->bhqk', qb, k).astype(jnp.float32) * scale
        q_pos = start + jnp.arange(Q_BLOCK)
        s = jnp.where(k_pos[None, :] <= q_pos[:, None], s, -jnp.inf)
        p = jax.nn.softmax(s, axis=-1).astype(v.dtype)
        return jnp.einsum('bhqk,bkhd->bqhd', p, v)

    out = lax.map(one_block, (q_blocks, jnp.arange(nq) * Q_BLOCK))
    return out.swapaxes(0, 1).reshape(Bsz, S, H, v.shape[-1])


def hybrid_attention_mixer(h, rope_a, rope_b, w_in, q_norm, w_uq, kv_norm, w_ukv, w_out):
    Bsz, S, _ = h.shape
    proj = h @ w_in
    cuts = [MIX_A, 2 * MIX_A, 3 * MIX_A, 3 * MIX_A + MLA_Q_RANK, 3 * MIX_A + MLA_Q_RANK + MLA_KV_RANK]
    qa, ka, va, cq, ckv, kr = jnp.split(proj, cuts, axis=-1)
    qa = partial_rope(qa.reshape(Bsz, S, N_HEADS_A, HEAD_DIM_A), *rope_a)
    ka = partial_rope(ka.reshape(Bsz, S, N_HEADS_A, HEAD_DIM_A), *rope_a)
    va = va.reshape(Bsz, S, N_HEADS_A, HEAD_DIM_A)
    o_a = dilated_mixture_attention(qa, ka, va).reshape(Bsz, S, MIX_A)
    q = (rmsnorm(cq, q_norm) @ w_uq).reshape(Bsz, S, MLA_HEADS, MLA_NOPE + MLA_ROPE)
    q = jnp.concatenate([q[..., :MLA_NOPE], apply_rope(q[..., MLA_NOPE:], *rope_b)], axis=-1)
    kv = (rmsnorm(ckv, kv_norm) @ w_ukv).reshape(Bsz, S, MLA_HEADS, MLA_NOPE + MLA_V)
    k_rope = apply_rope(kr.reshape(Bsz, S, 1, MLA_ROPE), *rope_b)
    k = jnp.concatenate([kv[..., :MLA_NOPE], jnp.broadcast_to(k_rope, (Bsz, S, MLA_HEADS, MLA_ROPE))], axis=-1)
    v = kv[..., MLA_NOPE:]
    o_b = causal_block_attention(q, k, v).reshape(Bsz, S, MIX_B)
    return jnp.concatenate([o_a, o_b], axis=-1) @ w_out


def causal_depthwise_conv(u, w, b):
    width, ch = w.shape
    y = lax.conv_general_dilated(u, w[:, None, :], window_strides=(1,), padding=[(width - 1, 0)],
                                 dimension_numbers=('NWC', 'WIO', 'NWC'), feature_group_count=ch)
    return y + b


def ssd_chunked_scan(xs, dt, a, bm, cm):
    Bsz, S, H, P = xs.shape
    G, N = bm.shape[2], bm.shape[3]
    hg = H // G
    Sp = -(-S // CHUNK) * CHUNK
    nc = Sp // CHUNK

    def chunks(t):
        t = jnp.pad(t, [(0, 0), (0, Sp - S)] + [(0, 0)] * (t.ndim - 2))
        return t.reshape((Bsz, nc, CHUNK) + t.shape[2:]).swapaxes(0, 1)

    xc = chunks(xs.astype(jnp.float32).reshape(Bsz, S, G, hg, P))
    dtc = chunks(dt.reshape(Bsz, S, G, hg))
    bc = chunks(bm.astype(jnp.float32))
    cc = chunks(cm.astype(jnp.float32))
    a_g = a.reshape(G, hg)
    causal = jnp.tril(jnp.ones((CHUNK, CHUNK), dtype=bool))

    def step(state, inp):
        x, dtq, bq, cq = inp
        cs = jnp.cumsum(dtq * a_g, axis=1)
        seg = cs[:, :, None] - cs[:, None, :]
        decay = jnp.exp(jnp.where(causal[None, :, :, None, None], seg, -jnp.inf))
        xdt = x * dtq[..., None]
        cb = jnp.einsum('bign,bjgn->bijg', cq, bq)
        y = jnp.einsum('bijgh,bjghp->bighp', cb[..., None] * decay, xdt)
        y = y + jnp.einsum('bign,bghpn->bighp', cq, state) * jnp.exp(cs)[..., None]
        to_end = jnp.exp(cs[:, -1:] - cs)
        state = state * jnp.exp(cs[:, -1])[..., None, None] + jnp.einsum('bjgn,bjghp->bghpn', bq, to_end[..., None] * xdt)
        return state, y

    state0 = jnp.zeros((Bsz, G, hg, P, N), jnp.float32)
    _, ys = lax.scan(step, state0, (xc, dtc, bc, cc))
    return ys.swapaxes(0, 1).reshape(Bsz, Sp, H, P)[:, :S]


def ssd_mixer(h, w_in, conv_w, conv_b, dt_bias, a_log, d_skip, gate_norm, w_out):
    Bsz, S, _ = h.shape
    z, xbc, dt = jnp.split(h @ w_in, [D_INNER, D_INNER + XBC_DIM], axis=-1)
    xbc = jax.nn.silu(causal_depthwise_conv(xbc, conv_w, conv_b))
    xs, bm, cm = jnp.split(xbc, [D_INNER, D_INNER + SSD_GROUPS * D_STATE], axis=-1)
    xs = xs.reshape(Bsz, S, SSD_HEADS, SSD_HEADDIM)
    bm = bm.reshape(Bsz, S, SSD_GROUPS, D_STATE)
    cm = cm.reshape(Bsz, S, SSD_GROUPS, D_STATE)
    dt = jax.nn.softplus(dt.astype(jnp.float32) + dt_bias.astype(jnp.float32))
    a = -jnp.exp(a_log.astype(jnp.float32))
    y = ssd_chunked_scan(xs, dt, a, bm, cm) + d_skip.astype(jnp.float32)[:, None] * xs.astype(jnp.float32)
    y = y.reshape(Bsz, S, D_INNER) * jax.nn.silu(z.astype(jnp.float32))
    y = rmsnorm(y.reshape(Bsz, S, SSD_GROUPS, D_INNER // SSD_GROUPS), gate_norm.reshape(SSD_GROUPS, -1))
    return y.reshape(Bsz, S, D_INNER).astype(h.dtype) @ w_out


def hierarchical_moe(h, w_group, b_group, w_router, b_router, w_gate, w_up, w_down):
    Bsz, S, D = h.shape
    T = Bsz * S
    t = h.reshape(T, D)
    g_prob = jax.nn.softmax((t @ w_group).astype(jnp.float32) + b_group.astype(jnp.float32), axis=-1)
    g_val, g_idx = lax.top_k(g_prob, 1)
    e_logits = ((t @ w_router).astype(jnp.float32) + b_router.astype(jnp.float32)).reshape(T, N_GROUPS, EXPERTS_PER_GROUP)
    in_group = e_logits[jnp.arange(T), g_idx[:, 0]]
    e_val, e_idx = lax.top_k(in_group, TOP_K_IN_GROUP)
    gates = jax.nn.softmax(e_val, axis=-1) * g_val
    experts = g_idx * EXPERTS_PER_GROUP + e_idx
    tk = T * TOP_K_IN_GROUP
    flat_e = experts.reshape(tk)
    flat_g = gates.reshape(tk)
    flat_tok = jnp.repeat(jnp.arange(T, dtype=jnp.int32), TOP_K_IN_GROUP)
    order = jnp.argsort(flat_e)
    se, stok, sg = flat_e[order], flat_tok[order], flat_g[order]
    counts = jnp.bincount(flat_e, length=N_EXPERTS)
    starts = jnp.cumsum(counts) - counts
    padded = (counts + MOE_BLOCK - 1) // MOE_BLOCK * MOE_BLOCK
    pad_ends = jnp.cumsum(padded)
    pad_starts = pad_ends - padded
    dest = pad_starts[se] + jnp.arange(tk) - starts[se]
    n_blocks = -(-tk // MOE_BLOCK) + N_EXPERTS
    rows = n_blocks * MOE_BLOCK
    row_tok = jnp.zeros((rows,), jnp.int32).at[dest].set(stok)
    row_gate = jnp.zeros((rows,), jnp.float32).at[dest].set(sg)
    block_expert = jnp.minimum(jnp.searchsorted(pad_ends, jnp.arange(n_blocks) * MOE_BLOCK, side='right'), N_EXPERTS - 1)

    def expert_block(args):
        toks, gts, e = args
        xb = t[toks]
        hid = jax.nn.silu(xb @ w_gate[e]) * (xb @ w_up[e])
        return (hid @ w_down[e]) * gts[:, None].astype(xb.dtype)

    y_rows = lax.map(expert_block, (row_tok.reshape(n_blocks, MOE_BLOCK), row_gate.reshape(n_blocks, MOE_BLOCK), block_expert))
    y = jnp.zeros((T, D), h.dtype).at[row_tok].add(y_rows.reshape(rows, D).astype(h.dtype))
    return y.reshape(Bsz, S, D)


def setup_inputs(seed: int = 0) -> dict:
    key = jax.random.key(seed)
    ks = iter(jax.random.split(key, 32))
    f32 = jnp.float32
    n_even = (DEPTH + 1) // 2
    n_odd = DEPTH // 2

    def normal(shape, scale):
        return jax.random.normal(next(ks), shape, f32) * scale

    def gain(shape):
        return 1.0 + normal(shape, 0.02)

    x = normal((BATCH, SEQ, D_MODEL), 1.0)
    start = jax.random.randint(next(ks), (BATCH, 1), 0, 4096, dtype=jnp.int32)
    positions = start + jnp.arange(SEQ, dtype=jnp.int32)[None, :]
    attn_norm = gain((n_even, D_MODEL))
    w_in_attn = normal((n_even, D_MODEL, ATTN_IN), D_MODEL ** -0.5)
    mla_q_norm = gain((n_even, MLA_Q_RANK))
    w_uq = normal((n_even, MLA_Q_RANK, MLA_HEADS * (MLA_NOPE + MLA_ROPE)), MLA_Q_RANK ** -0.5)
    mla_kv_norm = gain((n_even, MLA_KV_RANK))
    w_ukv = normal((n_even, MLA_KV_RANK, MLA_HEADS * (MLA_NOPE + MLA_V)), MLA_KV_RANK ** -0.5)
    w_out_attn = normal((n_even, MIX_A + MIX_B, D_MODEL), (MIX_A + MIX_B) ** -0.5)
    ssd_norm = gain((n_odd, D_MODEL))
    w_in_ssd = normal((n_odd, D_MODEL, SSD_IN), D_MODEL ** -0.5)
    conv_w = normal((n_odd, D_CONV, XBC_DIM), D_CONV ** -0.5)
    conv_b = normal((n_odd, XBC_DIM), 0.02)
    dt0 = jnp.exp(jax.random.uniform(next(ks), (n_odd, SSD_HEADS), f32, math.log(1e-3), math.log(1e-1)))
    dt_bias = dt0 + jnp.log(-jnp.expm1(-dt0))
    a_log = jnp.log(jax.random.uniform(next(ks), (n_odd, SSD_HEADS), f32, 1.0, 16.0))
    d_skip = gain((n_odd, SSD_HEADS))
    gate_norm = gain((n_odd, D_INNER))
    w_out_ssd = normal((n_odd, D_INNER, D_MODEL), D_INNER ** -0.5)
    moe_norm = gain((DEPTH, D_MODEL))
    w_group = normal((DEPTH, D_MODEL, N_GROUPS), D_MODEL ** -0.5)
    b_group = normal((DEPTH, N_GROUPS), 0.01)
    w_router = normal((DEPTH, D_MODEL, N_EXPERTS), D_MODEL ** -0.5)
    b_router = normal((DEPTH, N_EXPERTS), 0.01)
    w_gate = normal((DEPTH, N_EXPERTS, D_MODEL, D_EXPERT), D_MODEL ** -0.5)
    w_up = normal((DEPTH, N_EXPERTS, D_MODEL, D_EXPERT), D_MODEL ** -0.5)
    w_down = normal((DEPTH, N_EXPERTS, D_EXPERT, D_MODEL), D_EXPERT ** -0.5)
    final_norm = gain((D_MODEL,))
    return {'x': x, 'positions': positions, 'attn_norm': attn_norm, 'w_in_attn': w_in_attn,
            'mla_q_norm': mla_q_norm, 'w_uq': w_uq, 'mla_kv_norm': mla_kv_norm, 'w_ukv': w_ukv,
            'w_out_attn': w_out_attn, 'ssd_norm': ssd_norm, 'w_in_ssd': w_in_ssd, 'conv_w': conv_w,
            'conv_b': conv_b, 'dt_bias': dt_bias, 'a_log': a_log, 'd_skip': d_skip,
            'gate_norm': gate_norm, 'w_out_ssd': w_out_ssd, 'moe_norm': moe_norm,
            'w_group': w_group, 'b_group': b_group, 'w_router': w_router, 'b_router': b_router,
            'w_gate': w_gate, 'w_up': w_up, 'w_down': w_down, 'final_norm': final_norm}


def reference(x, positions, attn_norm, w_in_attn, mla_q_norm, w_uq, mla_kv_norm, w_ukv, w_out_attn,
              ssd_norm, w_in_ssd, conv_w, conv_b, dt_bias, a_log, d_skip, gate_norm, w_out_ssd,
              moe_norm, w_group, b_group, w_router, b_router, w_gate, w_up, w_down, final_norm):
    rope_a = rope_tables(positions, ROPE_DIM_A)
    rope_b = rope_tables(positions, MLA_ROPE)
    for layer in range(DEPTH):
        i = layer // 2
        if layer % 2 == 0:
            x = x + hybrid_attention_mixer(rmsnorm(x, attn_norm[i]), rope_a, rope_b, w_in_attn[i],
                                           mla_q_norm[i], w_uq[i], mla_kv_norm[i], w_ukv[i], w_out_attn[i])
        else:
            x = x + ssd_mixer(rmsnorm(x, ssd_norm[i]), w_in_ssd[i], conv_w[i], conv_b[i], dt_bias[i],
                              a_log[i], d_skip[i], gate_norm[i], w_out_ssd[i])
        x = x + hierarchical_moe(rmsnorm(x, moe_norm[layer]), w_group[layer], b_group[layer],
                                 w_router[layer], b_router[layer], w_gate[layer], w_up[layer], w_down[layer])
    return rmsnorm(x, final_norm)
```

```python
import functools

import jax
import jax.numpy as jnp
from jax import lax
from jax.experimental import pallas as pl
from jax.experimental.pallas import tpu as pltpu

F32 = jnp.float32
BF16 = jnp.bfloat16

D_MODEL = 1024
HEADS_A = 8
HEAD_DIM_A = 64
ROPE_DIM_A = 16
DILATIONS = (1, 4, 16)
SPAN = 128
MLA_HEADS = 8
MLA_NOPE = 64
MLA_ROPE = 32
MLA_V = 64
MLA_Q_RANK = 256
MLA_KV_RANK = 128
ROPE_THETA = 500000.0
MIX_A = HEADS_A * HEAD_DIM_A
MIX_B = MLA_HEADS * MLA_V
D_INNER = 2048
SSD_HEADDIM = 64
SSD_HEADS = 32
SSD_GROUPS = 4
D_STATE = 128
D_CONV = 4
CHUNK = 256
XBC_DIM = D_INNER + 2 * SSD_GROUPS * D_STATE
N_GROUPS = 4
EXPERTS_PER_GROUP = 8
N_EXPERTS = 32
D_EXPERT = 512
NORM_EPS = 1e-6

LANES = 128
VMEM_LIMIT = 56 * 1024 * 1024
ROW_TILE = 512
MOE_ROWS = 256
MLA_TQ = 512
MLA_TK = 512
NEG_INF = float("-inf")


def _params(*sem):
    return pltpu.CompilerParams(dimension_semantics=sem, vmem_limit_bytes=VMEM_LIMIT)


def _dot(a, b):
    return jnp.dot(a, b, preferred_element_type=F32)


def _dot_nt(a, b):
    return lax.dot_general(a, b, (((1,), (1,)), ((), ())), preferred_element_type=F32)


def _rms(x, g):
    return x * lax.rsqrt(jnp.mean(x * x, axis=-1, keepdims=True) + NORM_EPS) * g


def _sigmoid(x):
    return 1.0 / (1.0 + jnp.exp(-x))


def _rope(t, c, s1, s2, half):
    return t * c + pltpu.roll(t, LANES - half, 1) * s1 + pltpu.roll(t, half, 1) * s2


def _full(shape):
    return pl.BlockSpec(shape, lambda *_: (0,) * len(shape))


def _attn_pre_kernel(x_ref, g_ref, w_ref, qn_ref, wuq_ref, kvn_ref, wuk_ref, wuv_ref,
                     ca_ref, s1a_ref, s2a_ref, cb_ref, s1b_ref, s2b_ref,
                     qa_ref, ka_ref, va_ref, qm_ref, km_ref, vm_ref):
    xn = _rms(x_ref[...], g_ref[...]).astype(BF16)
    ca, s1a, s2a = ca_ref[...], s1a_ref[...], s2a_ref[...]
    cb, s1b, s2b = cb_ref[...], s1b_ref[...], s2b_ref[...]
    half_a = ROPE_DIM_A // 2
    half_b = MLA_ROPE // 2

    q = _dot(xn, w_ref[:, 0:MIX_A]) * (HEAD_DIM_A ** -0.5)
    k = _dot(xn, w_ref[:, MIX_A:2 * MIX_A])
    for c in range(MIX_A // LANES):
        sl = slice(c * LANES, (c + 1) * LANES)
        qa_ref[:, sl] = _rope(q[:, sl], ca, s1a, s2a, half_a).astype(BF16)
        ka_ref[:, sl] = _rope(k[:, sl], ca, s1a, s2a, half_a).astype(BF16)
    va_ref[...] = _dot(xn, w_ref[:, 2 * MIX_A:3 * MIX_A]).astype(BF16)

    o = 3 * MIX_A
    cq = _dot(xn, w_ref[:, o:o + MLA_Q_RANK])
    ckv = _dot(xn, w_ref[:, o + MLA_Q_RANK:o + MLA_Q_RANK + MLA_KV_RANK])
    kr = _dot(xn, w_ref[:, o + MLA_Q_RANK + MLA_KV_RANK:o + MLA_Q_RANK + MLA_KV_RANK + LANES])
    cqn = _rms(cq, qn_ref[...]).astype(BF16)
    kvn = _rms(ckv, kvn_ref[...]).astype(BF16)
    krr = _rope(kr, cb, s1b, s2b, half_b)
    qm = _dot(cqn, wuq_ref[...]) * ((MLA_NOPE + MLA_ROPE) ** -0.5)
    km = _dot(kvn, wuk_ref[...])
    for h in range(MLA_HEADS):
        sl = slice(h * LANES, (h + 1) * LANES)
        qm_ref[:, sl] = _rope(qm[:, sl], cb, s1b, s2b, half_b).astype(BF16)
        km_ref[:, sl] = (km[:, sl] + krr).astype(BF16)
    vm_ref[...] = _dot(kvn, wuv_ref[...]).astype(BF16)


def _attn_pre(x2, g, w_cat, qn, wuq, kvn, wuk, wuv, tabs):
    T = x2.shape[0]
    tm = min(ROW_TILE, T)
    row = lambda n: pl.BlockSpec((tm, n), lambda i: (i, 0))
    wide = MLA_HEADS * LANES
    return pl.pallas_call(
        _attn_pre_kernel,
        grid=(T // tm,),
        in_specs=[row(D_MODEL), _full((1, D_MODEL)), _full(w_cat.shape), _full((1, MLA_Q_RANK)),
                  _full(wuq.shape), _full((1, MLA_KV_RANK)), _full(wuk.shape), _full(wuv.shape)]
                 + [row(LANES)] * 6,
        out_specs=[row(MIX_A), row(MIX_A), row(MIX_A), row(wide), row(wide), row(MIX_B)],
        out_shape=[jax.ShapeDtypeStruct((T, MIX_A), BF16)] * 3
                  + [jax.ShapeDtypeStruct((T, wide), BF16)] * 2
                  + [jax.ShapeDtypeStruct((T, MIX_B), BF16)],
        compiler_params=_params("parallel"),
        name="attn_pre",
    )(x2, g, w_cat, qn, wuq, kvn, wuk, wuv, *tabs)


def _dilated_kernel(*refs, tq, carry):
    if carry:
        q_ref, kc_ref, kp_ref, vc_ref, vp_ref, op_ref, lp_ref, o_ref, lse_ref, kbuf, vbuf = refs
    else:
        q_ref, kc_ref, kp_ref, vc_ref, vp_ref, o_ref, lse_ref, kbuf, vbuf = refs
    j = pl.program_id(2)
    kbuf[0:SPAN, :] = kp_ref[0]
    kbuf[SPAN:, :] = kc_ref[0]
    vbuf[0:SPAN, :] = vp_ref[0]
    vbuf[SPAN:, :] = vc_ref[0]
    low = lax.broadcasted_iota(jnp.int32, (SPAN, LANES), 1) < HEAD_DIM_A
    qi = lax.broadcasted_iota(jnp.int32, (SPAN, 2 * SPAN), 0)
    ki = lax.broadcasted_iota(jnp.int32, (SPAN, 2 * SPAN), 1)
    band = (ki >= qi) & (ki <= qi + SPAN)
    first_valid = band & ((ki >= SPAN) | (j > 0))
    for sub in range(tq // SPAN):
        valid = first_valid if sub == 0 else band
        rows = slice(sub * SPAN, (sub + 1) * SPAN)
        krows = slice(sub * SPAN, (sub + 2) * SPAN)
        for p in range(MIX_A // LANES):
            lanes = slice(p * LANES, (p + 1) * LANES)
            q2 = q_ref[0, rows, lanes]
            kk = kbuf[krows, lanes]
            vv = vbuf[krows, lanes]
            os_, ls_ = [], []
            for e in range(2):
                qe = jnp.where(low if e == 0 else jnp.logical_not(low), q2, jnp.zeros_like(q2))
                s = jnp.where(valid, _dot_nt(qe, kk), NEG_INF)
                m = jnp.max(s, axis=-1, keepdims=True)
                pr = jnp.exp(s - m)
                l = jnp.sum(pr, axis=-1, keepdims=True)
                os_.append(_dot(pr.astype(BF16), vv) / l)
                ls_.append(m + jnp.log(l))
            o2 = jnp.where(low, os_[0], os_[1])
            l2 = jnp.where(low, ls_[0], ls_[1])
            if carry:
                lp = lp_ref[0, rows, lanes]
                mx = jnp.maximum(lp, l2)
                wp = jnp.exp(lp - mx)
                wc = jnp.exp(l2 - mx)
                den = wp + wc
                o2 = (wp * op_ref[0, rows, lanes].astype(F32) + wc * o2) / den
                l2 = mx + jnp.log(den)
            o_ref[0, rows, lanes] = o2.astype(BF16)
            lse_ref[0, rows, lanes] = l2


def _dilated_branch(q, k, v, d, prev):
    B, S, _ = q.shape
    L = S // d
    tq = min(512, L)
    view = lambda t: t.reshape(B, L, d * t.shape[-1])
    cur = pl.BlockSpec((1, tq, MIX_A), lambda b, r, j: (b, j, r))
    prv = pl.BlockSpec((1, SPAN, MIX_A), lambda b, r, j: (b, jnp.maximum(j * (tq // SPAN) - 1, 0), r))
    carry = prev is not None
    ins = [view(q), view(k), view(k), view(v), view(v)]
    in_specs = [cur, cur, prv, cur, prv]
    if carry:
        ins += [view(prev[0]), view(prev[1])]
        in_specs += [cur, cur]
    o, lse = pl.pallas_call(
        functools.partial(_dilated_kernel, tq=tq, carry=carry),
        grid=(B, d, L // tq),
        in_specs=in_specs,
        out_specs=[cur, cur],
        out_shape=[jax.ShapeDtypeStruct((B, L, d * MIX_A), BF16),
                   jax.ShapeDtypeStruct((B, L, d * MIX_A), F32)],
        scratch_shapes=[pltpu.VMEM((tq + SPAN, MIX_A), BF16), pltpu.VMEM((tq + SPAN, MIX_A), BF16)],
        compiler_params=_params("parallel", "parallel", "parallel"),
        name=f"dilated_d{d}",
    )(*ins)
    return o.reshape(B, S, MIX_A), lse.reshape(B, S, MIX_A)


def _mla_kernel(q_ref, k_ref, v_ref, o_ref, m_sc, l_sc, acc_sc, *, tq, tk):
    qi = pl.program_id(2)
    m_sc[...] = jnp.full(m_sc.shape, NEG_INF, F32)
    l_sc[...] = jnp.zeros(l_sc.shape, F32)
    acc_sc[...] = jnp.zeros(acc_sc.shape, F32)
    row = lax.broadcasted_iota(jnp.int32, (tq, tk), 0)
    col = lax.broadcasted_iota(jnp.int32, (tq, tk), 1)

    def step(kv, masked):
        start = pl.multiple_of(kv * tk, tk)
        v = v_ref[0, pl.ds(start, tk), :]
        for e in range(2):
            q = q_ref[0, :, e * LANES:(e + 1) * LANES]
            k = k_ref[0, pl.ds(start, tk), e * LANES:(e + 1) * LANES]
            s = _dot_nt(q, k)
            if masked:
                s = jnp.where(col <= row, s, NEG_INF)
            m_prev = m_sc[e]
            m_new = jnp.maximum(m_prev, jnp.max(s, axis=-1, keepdims=True))
            alpha = jnp.exp(m_prev - m_new)
            pr = jnp.exp(s - m_new)
            l_sc[e] = alpha * l_sc[e] + jnp.sum(pr, axis=-1, keepdims=True)
            acc_sc[e] = alpha * acc_sc[e] + _dot(pr.astype(BF16), v)
            m_sc[e] = m_new

    def body(kv, c):
        step(kv, False)
        return c

    lax.fori_loop(0, qi * (tq // tk), body, 0)
    for t in range(tq // tk):
        step(qi * (tq // tk) + t, True)
    low = lax.broadcasted_iota(jnp.int32, (tq, LANES), 1) < MLA_V
    o_ref[0] = jnp.where(low, acc_sc[0] / l_sc[0], acc_sc[1] / l_sc[1]).astype(BF16)


def _mla_attention(qm, km, vm):
    B, S, _ = qm.shape
    tq = min(MLA_TQ, S)
    tk = tq
    pairs = MLA_HEADS // 2
    return pl.pallas_call(
        functools.partial(_mla_kernel, tq=tq, tk=tk),
        grid=(B, pairs, S // tq),
        in_specs=[pl.BlockSpec((1, tq, 2 * LANES), lambda b, p, i: (b, i, p)),
                  pl.BlockSpec((1, S, 2 * LANES), lambda b, p, i: (b, 0, p)),
                  pl.BlockSpec((1, S, LANES), lambda b, p, i: (b, 0, p))],
        out_specs=pl.BlockSpec((1, tq, LANES), lambda b, p, i: (b, i, p)),
        out_shape=jax.ShapeDtypeStruct((B, S, MIX_B), BF16),
        scratch_shapes=[pltpu.VMEM((2, tq, 1), F32), pltpu.VMEM((2, tq, 1), F32),
                        pltpu.VMEM((2, tq, LANES), F32)],
        compiler_params=_params("parallel", "parallel", "arbitrary"),
        name="mla_flash",
    )(qm, km, vm)


def _route_tail(x1, gm_ref, wrh_ref, wrl_ref, br_ref, x1_ref, h_ref, ri_ref, rg_ref):
    x1_ref[...] = x1
    hn = _rms(x1, gm_ref[...])
    h_hi = hn.astype(BF16)
    h_ref[...] = h_hi
    h_lo = (hn - h_hi.astype(F32)).astype(BF16)
    logits = (_dot(h_hi, wrh_ref[...]) + _dot(h_hi, wrl_ref[...]) + _dot(h_lo, wrh_ref[...])
              + br_ref[...])
    tm = logits.shape[0]
    lane = lax.broadcasted_iota(jnp.int32, (tm, LANES), 1)
    lanef = lane.astype(F32)
    big = float(LANES)
    is_g = (lane >= N_EXPERTS) & (lane < N_EXPERTS + N_GROUPS)
    gl = jnp.where(is_g, logits, NEG_INF)
    gmax = jnp.max(gl, axis=-1, keepdims=True)
    gsum = jnp.sum(jnp.exp(gl - gmax), axis=-1, keepdims=True)
    g_val = 1.0 / gsum
    g_idx = jnp.min(jnp.where(gl == gmax, lanef, big), axis=-1, keepdims=True) - float(N_EXPERTS)
    in_grp = (lane < N_EXPERTS) & ((lane // EXPERTS_PER_GROUP).astype(F32) == g_idx)
    el = jnp.where(in_grp, logits, NEG_INF)
    e1 = jnp.max(el, axis=-1, keepdims=True)
    i1 = jnp.min(jnp.where(el == e1, lanef, big), axis=-1, keepdims=True)
    el2 = jnp.where(lanef == i1, NEG_INF, el)
    e2 = jnp.max(el2, axis=-1, keepdims=True)
    i2 = jnp.min(jnp.where(el2 == e2, lanef, big), axis=-1, keepdims=True)
    t = jnp.exp(e2 - e1)
    p1 = 1.0 / (1.0 + t)
    ri_ref[...] = jnp.where(lane == 0, i1, jnp.where(lane == 1, i2, 0.0)).astype(jnp.int32)
    rg_ref[...] = jnp.where(lane == 0, p1 * g_val, jnp.where(lane == 1, t * p1 * g_val, 0.0))


def _tail_specs(tm):
    row = lambda n: pl.BlockSpec((tm, n), lambda i: (i, 0))
    in_specs = [_full((1, D_MODEL)), _full((D_MODEL, LANES)), _full((D_MODEL, LANES)), _full((1, LANES))]
    out_specs = [row(D_MODEL), row(D_MODEL), row(LANES), row(LANES)]
    return in_specs, out_specs


def _tail_shapes(T):
    return [jax.ShapeDtypeStruct((T, D_MODEL), F32), jax.ShapeDtypeStruct((T, D_MODEL), BF16),
            jax.ShapeDtypeStruct((T, LANES), jnp.int32), jax.ShapeDtypeStruct((T, LANES), F32)]


def _attn_post_kernel(oa_ref, ob_ref, x_ref, wo_ref, gm_ref, wrh_ref, wrl_ref, br_ref,
                      x1_ref, h_ref, ri_ref, rg_ref):
    y = _dot(oa_ref[...], wo_ref[0:MIX_A, :]) + _dot(ob_ref[...], wo_ref[MIX_A:, :])
    _route_tail(x_ref[...] + y, gm_ref, wrh_ref, wrl_ref, br_ref, x1_ref, h_ref, ri_ref, rg_ref)


def _attn_post(oa, ob, x2, wo, tail):
    T = x2.shape[0]
    tm = min(ROW_TILE, T)
    row = lambda n: pl.BlockSpec((tm, n), lambda i: (i, 0))
    t_in, t_out = _tail_specs(tm)
    return pl.pallas_call(
        _attn_post_kernel,
        grid=(T // tm,),
        in_specs=[row(MIX_A), row(MIX_B), row(D_MODEL), _full(wo.shape)] + t_in,
        out_specs=t_out,
        out_shape=_tail_shapes(T),
        compiler_params=_params("parallel"),
        name="attn_post",
    )(oa, ob, x2, wo, *tail)


def _expert_kernel(be_ref, nv_ref, xs_ref, wg_ref, wu_ref, wd_ref, y_ref):
    i = pl.program_id(0)

    @pl.when(i < nv_ref[0])
    def _():
        xb = xs_ref[...]
        g = _dot(xb, wg_ref[0])
        u = _dot(xb, wu_ref[0])
        hid = (g * _sigmoid(g) * u).astype(BF16)
        y_ref[...] = _dot(hid, wd_ref[0]).astype(y_ref.dtype)

    @pl.when(i >= nv_ref[0])
    def _():
        y_ref[...] = jnp.zeros(y_ref.shape, y_ref.dtype)


def _expert_blocks(xs, block_expert, n_valid, wg, wu, wd):
    rows = xs.shape[0]
    bm = MOE_ROWS
    last = lambda i, be, nv: jnp.minimum(i, nv[0] - 1)
    grid_spec = pltpu.PrefetchScalarGridSpec(
        num_scalar_prefetch=2,
        grid=(rows // bm,),
        in_specs=[pl.BlockSpec((bm, D_MODEL), lambda i, be, nv: (last(i, be, nv), 0)),
                  pl.BlockSpec((1, D_MODEL, D_EXPERT), lambda i, be, nv: (be[i], 0, 0)),
                  pl.BlockSpec((1, D_MODEL, D_EXPERT), lambda i, be, nv: (be[i], 0, 0)),
                  pl.BlockSpec((1, D_EXPERT, D_MODEL), lambda i, be, nv: (be[i], 0, 0))],
        out_specs=pl.BlockSpec((bm, D_MODEL), lambda i, be, nv: (i, 0)),
    )
    return pl.pallas_call(
        _expert_kernel,
        grid_spec=grid_spec,
        out_shape=jax.ShapeDtypeStruct((rows, D_MODEL), BF16),
        compiler_params=_params("arbitrary"),
        name="moe_experts",
    )(block_expert, n_valid, xs, wg, wu, wd)


def _combine_kernel(x_ref, r0_ref, r1_ref, rg_ref, gf_ref, o_ref, *, final):
    rg = rg_ref[...]
    y = x_ref[...] + rg[:, 0:1] * r0_ref[...].astype(F32) + rg[:, 1:2] * r1_ref[...].astype(F32)
    o_ref[...] = _rms(y, gf_ref[...]) if final else y


def _moe_combine(x1, r0, r1, rg, gfinal, final):
    T = x1.shape[0]
    tm = min(ROW_TILE, T)
    row = lambda n: pl.BlockSpec((tm, n), lambda i: (i, 0))
    return pl.pallas_call(
        functools.partial(_combine_kernel, final=final),
        grid=(T // tm,),
        in_specs=[row(D_MODEL), row(D_MODEL), row(D_MODEL), row(LANES), _full((1, D_MODEL))],
        out_specs=row(D_MODEL),
        out_shape=jax.ShapeDtypeStruct((T, D_MODEL), F32),
        compiler_params=_params("parallel"),
        name="moe_combine",
    )(x1, r0, r1, rg, gfinal)


def _moe(x1, h, ri, rg, wg, wu, wd, gfinal, final):
    T = x1.shape[0]
    bm = MOE_ROWS
    e_flat = ri[:, :2].reshape(2 * T)
    onehot = (e_flat[:, None] == jnp.arange(N_EXPERTS, dtype=jnp.int32)[None, :]).astype(jnp.int32)
    csum = jnp.cumsum(onehot, axis=0)
    counts = csum[-1]
    rank = jnp.sum((csum - onehot) * onehot, axis=1)
    padded = (counts + bm - 1) // bm * bm
    pad_ends = jnp.cumsum(padded)
    pad_starts = pad_ends - padded
    dest = pad_starts[e_flat] + rank
    n_blocks = (2 * T) // bm + N_EXPERTS
    rows = n_blocks * bm
    tok = jnp.arange(2 * T, dtype=jnp.int32) // 2
    row_tok = jnp.zeros((rows,), jnp.int32).at[dest].set(tok)
    block_expert = jnp.minimum(
        jnp.searchsorted(pad_ends, jnp.arange(n_blocks, dtype=jnp.int32) * bm, side="right"),
        N_EXPERTS - 1).astype(jnp.int32)
    n_valid = (pad_ends[-1:] // bm).astype(jnp.int32)
    xs = jnp.take(h, row_tok, axis=0)
    y_rows = _expert_blocks(xs, block_expert, n_valid, wg, wu, wd)
    dest2 = dest.reshape(T, 2)
    r0 = jnp.take(y_rows, dest2[:, 0], axis=0)
    r1 = jnp.take(y_rows, dest2[:, 1], axis=0)
    return _moe_combine(x1, r0, r1, rg, gfinal, final)


def _ssd_pre_kernel(x_ref, g_ref, w_ref, z_ref, xbc_ref, dt_ref):
    xn = _rms(x_ref[...], g_ref[...]).astype(BF16)
    step = 512
    for c in range(0, D_INNER, step):
        z_ref[:, c:c + step] = _dot(xn, w_ref[:, c:c + step]).astype(BF16)
    for c in range(0, XBC_DIM, step):
        xbc_ref[:, c:c + step] = _dot(xn, w_ref[:, D_INNER + c:D_INNER + c + step]).astype(BF16)
    dt_ref[...] = _dot(xn, w_ref[:, D_INNER + XBC_DIM:])


def _ssd_pre(x2, g, w_cat):
    T = x2.shape[0]
    tm = min(ROW_TILE, T)
    row = lambda n: pl.BlockSpec((tm, n), lambda i: (i, 0))
    return pl.pallas_call(
        _ssd_pre_kernel,
        grid=(T // tm,),
        in_specs=[row(D_MODEL), _full((1, D_MODEL)), _full(w_cat.shape)],
        out_specs=[row(D_INNER), row(XBC_DIM), row(LANES)],
        out_shape=[jax.ShapeDtypeStruct((T, D_INNER), BF16), jax.ShapeDtypeStruct((T, XBC_DIM), BF16),
                   jax.ShapeDtypeStruct((T, LANES), F32)],
        compiler_params=_params("parallel"),
        name="ssd_pre",
    )(x2, g, w_cat)


def _pair_expand(v, j, low):
    return jnp.where(low, v[:, 2 * j:2 * j + 1], v[:, 2 * j + 1:2 * j + 2])


def _ssd_scan_kernel(xbc_ref, dt_ref, cw_ref, cb_ref, dtb_ref, alog_ref, dsk_ref, y_ref,
                     ubuf, abuf, state):
    Q = CHUNK
    c = pl.program_id(1)

    @pl.when(c == 0)
    def _():
        ubuf[0:8, :] = jnp.zeros((8, XBC_DIM), F32)
        state[...] = jnp.zeros(state.shape, F32)

    ubuf[8:8 + Q, :] = xbc_ref[0].astype(F32)
    step = 512
    for cc in range(0, XBC_DIM, step):
        sl = slice(cc, cc + step)
        acc = cb_ref[:, sl] + cw_ref[3:4, sl] * ubuf[8:8 + Q, sl]
        for kk in range(D_CONV - 1):
            acc = acc + cw_ref[kk:kk + 1, sl] * ubuf[5 + kk:5 + kk + Q, sl]
        abuf[:, sl] = acc * _sigmoid(acc)
    ubuf[0:8, :] = ubuf[Q:Q + 8, :]

    x_dt = dt_ref[0] + dtb_ref[...]
    dt = jnp.maximum(x_dt, 0.0) + jnp.log(1.0 + jnp.exp(-jnp.abs(x_dt)))
    da = dt * (-jnp.exp(alog_ref[...]))
    ri = lax.broadcasted_iota(jnp.int32, (Q, Q), 0)
    ci = lax.broadcasted_iota(jnp.int32, (Q, Q), 1)
    causal = ci <= ri
    tril = jnp.where(causal, 1.0, 0.0).astype(BF16)
    d_hi = da.astype(BF16)
    r1 = da - d_hi.astype(F32)
    d_mid = r1.astype(BF16)
    d_lo = (r1 - d_mid.astype(F32)).astype(BF16)
    cs = _dot(tril, d_hi) + _dot(tril, d_mid) + _dot(tril, d_lo)
    cs_t = jnp.transpose(cs)
    cs_last = cs[Q - 1:Q, :]
    ecs = jnp.exp(cs)
    to_end = jnp.exp(cs_last - cs)
    dec_last = jnp.exp(cs_last)
    low = lax.broadcasted_iota(jnp.int32, (Q, LANES), 1) < SSD_HEADDIM
    low1 = low[0:1, :]
    hg = SSD_HEADS // SSD_GROUPS
    gw = hg * SSD_HEADDIM
    for g in range(SSD_GROUPS):
        b_g = abuf[:, D_INNER + g * D_STATE:D_INNER + (g + 1) * D_STATE]
        c_g = abuf[:, D_INNER + SSD_GROUPS * D_STATE + g * D_STATE:
                   D_INNER + SSD_GROUPS * D_STATE + (g + 1) * D_STATE].astype(BF16)
        cb = _dot_nt(c_g, b_g.astype(BF16))
        b_t = jnp.transpose(b_g).astype(BF16)
        st = state[g]
        y_inter = _dot(c_g, st.astype(BF16))
        xw_parts, dl_parts = [], []
        for p in range(hg // 2):
            j = g * (hg // 2) + p
            lanes = slice(j * LANES, (j + 1) * LANES)
            x_p = abuf[:, lanes]
            xdt = x_p * _pair_expand(dt, j, low)
            ys = []
            for e in range(2):
                h = 2 * j + e
                seg = cs[:, h:h + 1] - cs_t[h:h + 1, :]
                lmat = jnp.exp(jnp.where(causal, seg, NEG_INF))
                ys.append(_dot((cb * lmat).astype(BF16), xdt.astype(BF16)))
            y_p = (jnp.where(low, ys[0], ys[1])
                   + y_inter[:, p * LANES:(p + 1) * LANES] * _pair_expand(ecs, j, low)
                   + dsk_ref[:, lanes] * x_p)
            y_ref[0, :, lanes] = y_p.astype(y_ref.dtype)
            xw_parts.append((xdt * _pair_expand(to_end, j, low)).astype(BF16))
            dl_parts.append(_pair_expand(dec_last, j, low1))
        xw = jnp.concatenate(xw_parts, axis=1)
        dl = jnp.concatenate(dl_parts, axis=1)
        state[g] = st * dl + _dot(b_t, xw)
    del gw


def _ssd_scan(xbc, dt, conv_w, conv_b, dt_bias, a_log, d_skip):
    B, S, _ = xbc.shape
    Q = CHUNK
    hg = SSD_HEADS // SSD_GROUPS
    return pl.pallas_call(
        _ssd_scan_kernel,
        grid=(B, S // Q),
        in_specs=[pl.BlockSpec((1, Q, XBC_DIM), lambda b, c: (b, c, 0)),
                  pl.BlockSpec((1, Q, LANES), lambda b, c: (b, c, 0)),
                  _full((D_CONV, XBC_DIM)), _full((1, XBC_DIM)), _full((1, LANES)), _full((1, LANES)),
                  _full((1, D_INNER))],
        out_specs=pl.BlockSpec((1, Q, D_INNER), lambda b, c: (b, c, 0)),
        out_shape=jax.ShapeDtypeStruct((B, S, D_INNER), BF16),
        scratch_shapes=[pltpu.VMEM((Q + 8, XBC_DIM), F32), pltpu.VMEM((Q, XBC_DIM), F32),
                        pltpu.VMEM((SSD_GROUPS, D_STATE, hg * SSD_HEADDIM), F32)],
        compiler_params=_params("parallel", "arbitrary"),
        name="ssd_scan",
    )(xbc, dt, conv_w, conv_b, dt_bias, a_log, d_skip)


def _ssd_post_kernel(y_ref, z_ref, gn_ref, x_ref, wo_ref, gm_ref, wrh_ref, wrl_ref, br_ref,
                     x1_ref, h_ref, ri_ref, rg_ref):
    gsz = D_INNER // SSD_GROUPS
    out = None
    for g in range(SSD_GROUPS):
        sl = slice(g * gsz, (g + 1) * gsz)
        z = z_ref[:, sl].astype(F32)
        yz = y_ref[:, sl].astype(F32) * (z * _sigmoid(z))
        part = _dot(_rms(yz, gn_ref[:, sl]).astype(BF16), wo_ref[sl, :])
        out = part if out is None else out + part
    _route_tail(x_ref[...] + out, gm_ref, wrh_ref, wrl_ref, br_ref, x1_ref, h_ref, ri_ref, rg_ref)


def _ssd_post(y, z, gn, x2, wo, tail):
    T = x2.shape[0]
    tm = min(ROW_TILE, T)
    row = lambda n: pl.BlockSpec((tm, n), lambda i: (i, 0))
    t_in, t_out = _tail_specs(tm)
    return pl.pallas_call(
        _ssd_post_kernel,
        grid=(T // tm,),
        in_specs=[row(D_INNER), row(D_INNER), _full((1, D_INNER)), row(D_MODEL), _full(wo.shape)] + t_in,
        out_specs=t_out,
        out_shape=_tail_shapes(T),
        compiler_params=_params("parallel"),
        name="ssd_post",
    )(y, z, gn, x2, wo, *tail)


def _rope_tables(positions):
    pos = positions.reshape(-1).astype(F32)[:, None]
    T = pos.shape[0]

    def tables(rot, lead, period):
        half = rot // 2
        inv = ROPE_THETA ** (-jnp.arange(0, rot, 2, dtype=F32) / rot)
        ang = pos * inv
        cos, sin = jnp.cos(ang), jnp.sin(ang)
        one = lambda n: jnp.ones((T, n), F32)
        zero = lambda n: jnp.zeros((T, n), F32)
        tail = period - lead - rot
        c = jnp.concatenate([one(lead), cos, cos, one(tail)], axis=1)
        s1 = jnp.concatenate([zero(lead), -sin, zero(half), zero(tail)], axis=1)
        s2 = jnp.concatenate([zero(lead), zero(half), sin, zero(tail)], axis=1)
        rep = LANES // period
        return [jnp.tile(t, (1, rep)) for t in (c, s1, s2)]

    return tables(ROPE_DIM_A, 0, HEAD_DIM_A) + tables(MLA_ROPE, MLA_NOPE, LANES)


def _pad_cols(w, n):
    return jnp.pad(w, ((0, 0), (0, n - w.shape[1])))


def _head_pad(w, width):
    K = w.shape[0]
    w = w.reshape(K, -1, width)
    return jnp.pad(w, ((0, 0), (0, 0), (0, LANES - width))).reshape(K, -1)


def _router_weights(w_group, b_group, w_router, b_router):
    w = _pad_cols(jnp.concatenate([w_router, w_group], axis=1), LANES)
    b = _pad_cols(jnp.concatenate([b_router, b_group])[None, :], LANES)
    hi = w.astype(BF16)
    lo = (w - hi.astype(F32)).astype(BF16)
    return hi, lo, b


def kernel(x, positions, attn_norm, w_in_attn, mla_q_norm, w_uq, mla_kv_norm, w_ukv, w_out_attn,
           ssd_norm, w_in_ssd, conv_w, conv_b, dt_bias, a_log, d_skip, gate_norm, w_out_ssd,
           moe_norm, w_group, b_group, w_router, b_router, w_gate, w_up, w_down, final_norm):
    B, S, D = x.shape
    T = B * S
    x2 = x.reshape(T, D)
    tabs = _rope_tables(positions)

    def tail_params(layer):
        hi, lo, b = _router_weights(w_group[layer], b_group[layer], w_router[layer], b_router[layer])
        return [moe_norm[layer][None, :], hi, lo, b]

    def moe_weights(layer):
        return w_gate[layer].astype(BF16), w_up[layer].astype(BF16), w_down[layer].astype(BF16)

    w_in = w_in_attn[0]
    kr_off = 3 * MIX_A + MLA_Q_RANK + MLA_KV_RANK
    kr_cols = jnp.pad(w_in[:, kr_off:], ((0, 0), (MLA_NOPE, LANES - MLA_NOPE - MLA_ROPE)))
    w_cat = jnp.concatenate([w_in[:, :kr_off], kr_cols], axis=1).astype(BF16)
    wuq = _head_pad(w_uq[0], MLA_NOPE + MLA_ROPE).astype(BF16)
    ukv = w_ukv[0].reshape(MLA_KV_RANK, MLA_HEADS, MLA_NOPE + MLA_V)
    wuk = _head_pad(ukv[:, :, :MLA_NOPE].reshape(MLA_KV_RANK, -1), MLA_NOPE).astype(BF16)
    wuv = ukv[:, :, MLA_NOPE:].reshape(MLA_KV_RANK, -1).astype(BF16)
    qa, ka, va, qm, km, vm = _attn_pre(x2, attn_norm[0][None, :], w_cat, mla_q_norm[0][None, :], wuq,
                                       mla_kv_norm[0][None, :], wuk, wuv, tabs)
    shp = lambda t: t.reshape(B, S, t.shape[-1])
    prev = None
    for d in DILATIONS:
        prev = _dilated_branch(shp(qa), shp(ka), shp(va), d, prev)
    oa = prev[0].reshape(T, MIX_A)
    ob = _mla_attention(shp(qm), shp(km), shp(vm)).reshape(T, MIX_B)
    x1, h, ri, rg = _attn_post(oa, ob, x2, w_out_attn[0].astype(BF16), tail_params(0))
    x2 = _moe(x1, h, ri, rg, *moe_weights(0), final_norm[None, :], False)

    w_cat = _pad_cols(w_in_ssd[0], D_INNER + XBC_DIM + LANES).astype(BF16)
    z, xbc, dt = _ssd_pre(x2, ssd_norm[0][None, :], w_cat)
    y = _ssd_scan(xbc.reshape(B, S, XBC_DIM), dt.reshape(B, S, LANES), conv_w[0], conv_b[0][None, :],
                  _pad_cols(dt_bias[0][None, :], LANES), _pad_cols(a_log[0][None, :], LANES),
                  jnp.repeat(d_skip[0], SSD_HEADDIM)[None, :])
    x1, h, ri, rg = _ssd_post(y.reshape(T, D_INNER), z, gate_norm[0][None, :], x2,
                              w_out_ssd[0].astype(BF16), tail_params(1))
    out = _moe(x1, h, ri, rg, *moe_weights(1), final_norm[None, :], True)
    return out.reshape(B, S, D)
```

```python
import functools

import jax
import jax.numpy as jnp
from jax import lax
from jax.experimental import pallas as pl
from jax.experimental.pallas import tpu as pltpu

F32 = jnp.float32
BF16 = jnp.bfloat16

D_MODEL = 1024
HEADS_A = 8
HEAD_DIM_A = 64
ROPE_DIM_A = 16
DILATIONS = (1, 4, 16)
SPAN = 128
MLA_HEADS = 8
MLA_NOPE = 64
MLA_ROPE = 32
MLA_V = 64
MLA_Q_RANK = 256
MLA_KV_RANK = 128
ROPE_THETA = 500000.0
MIX_A = HEADS_A * HEAD_DIM_A
MIX_B = MLA_HEADS * MLA_V
D_INNER = 2048
SSD_HEADDIM = 64
SSD_HEADS = 32
SSD_GROUPS = 4
D_STATE = 128
D_CONV = 4
CHUNK = 256
XBC_DIM = D_INNER + 2 * SSD_GROUPS * D_STATE
N_GROUPS = 4
EXPERTS_PER_GROUP = 8
N_EXPERTS = 32
D_EXPERT = 512
NORM_EPS = 1e-6

LANES = 128
VMEM_LIMIT = 56 * 1024 * 1024
ROW_TILE = 512
MOE_ROWS = 256
MLA_TQ = 512
MLA_TK = 512
NEG_INF = float("-inf")
LOG2E = 1.4426950408889634


def _params(*sem):
    return pltpu.CompilerParams(dimension_semantics=sem, vmem_limit_bytes=VMEM_LIMIT)


def _dot(a, b):
    return jnp.dot(a, b, preferred_element_type=F32)


def _dot_nt(a, b):
    return lax.dot_general(a, b, (((1,), (1,)), ((), ())), preferred_element_type=F32)


def _rms(x, g):
    return x * lax.rsqrt(jnp.mean(x * x, axis=-1, keepdims=True) + NORM_EPS) * g


def _sigmoid(x):
    return 1.0 / (1.0 + jnp.exp(-x))


def _rope(t, c, s1, s2, half):
    return t * c + pltpu.roll(t, LANES - half, 1) * s1 + pltpu.roll(t, half, 1) * s2


def _full(shape):
    return pl.BlockSpec(shape, lambda *_: (0,) * len(shape))


def _attn_pre_kernel(x_ref, g_ref, w_ref, qn_ref, wuq_ref, kvn_ref, wuk_ref, wuv_ref,
                     ca_ref, s1a_ref, s2a_ref, cb_ref, s1b_ref, s2b_ref,
                     qa_ref, ka_ref, va_ref, qm_ref, km_ref, vm_ref):
    xn = _rms(x_ref[...], g_ref[...]).astype(BF16)
    ca, s1a, s2a = ca_ref[...], s1a_ref[...], s2a_ref[...]
    cb, s1b, s2b = cb_ref[...], s1b_ref[...], s2b_ref[...]
    half_a = ROPE_DIM_A // 2
    half_b = MLA_ROPE // 2

    q = _dot(xn, w_ref[:, 0:MIX_A]) * (HEAD_DIM_A ** -0.5)
    k = _dot(xn, w_ref[:, MIX_A:2 * MIX_A])
    for c in range(MIX_A // LANES):
        sl = slice(c * LANES, (c + 1) * LANES)
        qa_ref[:, sl] = _rope(q[:, sl], ca, s1a, s2a, half_a).astype(BF16)
        ka_ref[:, sl] = _rope(k[:, sl], ca, s1a, s2a, half_a).astype(BF16)
    va_ref[...] = _dot(xn, w_ref[:, 2 * MIX_A:3 * MIX_A]).astype(BF16)

    o = 3 * MIX_A
    cq = _dot(xn, w_ref[:, o:o + MLA_Q_RANK])
    ckv = _dot(xn, w_ref[:, o + MLA_Q_RANK:o + MLA_Q_RANK + MLA_KV_RANK])
    kr = _dot(xn, w_ref[:, o + MLA_Q_RANK + MLA_KV_RANK:o + MLA_Q_RANK + MLA_KV_RANK + LANES])
    cqn = _rms(cq, qn_ref[...]).astype(BF16)
    kvn = _rms(ckv, kvn_ref[...]).astype(BF16)
    krr = _rope(kr, cb, s1b, s2b, half_b)
    qm = _dot(cqn, wuq_ref[...]) * ((MLA_NOPE + MLA_ROPE) ** -0.5 * LOG2E)
    km = _dot(kvn, wuk_ref[...])
    for h in range(MLA_HEADS):
        sl = slice(h * LANES, (h + 1) * LANES)
        qm_ref[:, sl] = _rope(qm[:, sl], cb, s1b, s2b, half_b).astype(BF16)
        km_ref[:, sl] = (km[:, sl] + krr).astype(BF16)
    vm = _dot(kvn, wuv_ref[...])
    low = lax.broadcasted_iota(jnp.int32, vm.shape, 1) % LANES < MLA_V
    vm_ref[...] = jnp.where(low, vm, 1.0).astype(BF16)


def _attn_pre(x2, g, w_cat, qn, wuq, kvn, wuk, wuv, tabs):
    T = x2.shape[0]
    tm = min(ROW_TILE, T)
    row = lambda n: pl.BlockSpec((tm, n), lambda i: (i, 0))
    wide = MLA_HEADS * LANES
    return pl.pallas_call(
        _attn_pre_kernel,
        grid=(T // tm,),
        in_specs=[row(D_MODEL), _full((1, D_MODEL)), _full(w_cat.shape), _full((1, MLA_Q_RANK)),
                  _full(wuq.shape), _full((1, MLA_KV_RANK)), _full(wuk.shape), _full(wuv.shape)]
                 + [row(LANES)] * 6,
        out_specs=[row(MIX_A), row(MIX_A), row(MIX_A), row(wide), row(wide), row(wide)],
        out_shape=[jax.ShapeDtypeStruct((T, MIX_A), BF16)] * 3
                  + [jax.ShapeDtypeStruct((T, wide), BF16)] * 3,
        compiler_params=_params("parallel"),
        name="attn_pre",
    )(x2, g, w_cat, qn, wuq, kvn, wuk, wuv, *tabs)


def _dilated_kernel(*refs, tq, carry):
    if carry:
        q_ref, kc_ref, kp_ref, vc_ref, vp_ref, op_ref, lp_ref, o_ref, lse_ref, kbuf, vbuf = refs
    else:
        q_ref, kc_ref, kp_ref, vc_ref, vp_ref, o_ref, lse_ref, kbuf, vbuf = refs
    j = pl.program_id(2)
    kbuf[0:SPAN, :] = kp_ref[0]
    kbuf[SPAN:, :] = kc_ref[0]
    vbuf[0:SPAN, :] = vp_ref[0]
    vbuf[SPAN:, :] = vc_ref[0]
    low = lax.broadcasted_iota(jnp.int32, (SPAN, LANES), 1) < HEAD_DIM_A
    qi = lax.broadcasted_iota(jnp.int32, (SPAN, 2 * SPAN), 0)
    ki = lax.broadcasted_iota(jnp.int32, (SPAN, 2 * SPAN), 1)
    band = (ki >= qi) & (ki <= qi + SPAN)
    first_valid = band & ((ki >= SPAN) | (j > 0))
    for sub in range(tq // SPAN):
        valid = first_valid if sub == 0 else band
        rows = slice(sub * SPAN, (sub + 1) * SPAN)
        krows = slice(sub * SPAN, (sub + 2) * SPAN)
        for p in range(MIX_A // LANES):
            lanes = slice(p * LANES, (p + 1) * LANES)
            q2 = q_ref[0, rows, lanes]
            kk = kbuf[krows, lanes]
            vv = vbuf[krows, lanes]
            os_, ls_ = [], []
            for e in range(2):
                qe = jnp.where(low if e == 0 else jnp.logical_not(low), q2, jnp.zeros_like(q2))
                s = jnp.where(valid, _dot_nt(qe, kk), NEG_INF)
                m = jnp.max(s, axis=-1, keepdims=True)
                pr = jnp.exp(s - m)
                l = jnp.sum(pr, axis=-1, keepdims=True)
                os_.append(_dot(pr.astype(BF16), vv) / l)
                ls_.append(m + jnp.log(l))
            o2 = jnp.where(low, os_[0], os_[1])
            l2 = jnp.where(low, ls_[0], ls_[1])
            if carry:
                lp = lp_ref[0, rows, lanes]
                mx = jnp.maximum(lp, l2)
                wp = jnp.exp(lp - mx)
                wc = jnp.exp(l2 - mx)
                den = wp + wc
                o2 = (wp * op_ref[0, rows, lanes].astype(F32) + wc * o2) / den
                l2 = mx + jnp.log(den)
            o_ref[0, rows, lanes] = o2.astype(BF16)
            lse_ref[0, rows, lanes] = l2


def _dilated_branch(q, k, v, d, prev):
    B, S, _ = q.shape
    L = S // d
    tq = min(512, L)
    view = lambda t: t.reshape(B, L, d * t.shape[-1])
    cur = pl.BlockSpec((1, tq, MIX_A), lambda b, r, j: (b, j, r))
    prv = pl.BlockSpec((1, SPAN, MIX_A), lambda b, r, j: (b, jnp.maximum(j * (tq // SPAN) - 1, 0), r))
    carry = prev is not None
    ins = [view(q), view(k), view(k), view(v), view(v)]
    in_specs = [cur, cur, prv, cur, prv]
    if carry:
        ins += [view(prev[0]), view(prev[1])]
        in_specs += [cur, cur]
    o, lse = pl.pallas_call(
        functools.partial(_dilated_kernel, tq=tq, carry=carry),
        grid=(B, d, L // tq),
        in_specs=in_specs,
        out_specs=[cur, cur],
        out_shape=[jax.ShapeDtypeStruct((B, L, d * MIX_A), BF16),
                   jax.ShapeDtypeStruct((B, L, d * MIX_A), F32)],
        scratch_shapes=[pltpu.VMEM((tq + SPAN, MIX_A), BF16), pltpu.VMEM((tq + SPAN, MIX_A), BF16)],
        compiler_params=_params("parallel", "parallel", "parallel"),
        name=f"dilated_d{d}",
    )(*ins)
    return o.reshape(B, S, MIX_A), lse.reshape(B, S, MIX_A)


def _mla_kernel(q_ref, k_ref, v_ref, o_ref, m0, m1, a0, a1, *, tq, tk):
    qi = pl.program_id(2)
    ms, accs = (m0, m1), (a0, a1)
    for e in range(2):
        ms[e][...] = jnp.full(ms[e].shape, NEG_INF, F32)
        accs[e][...] = jnp.zeros(accs[e].shape, F32)
    row = lax.broadcasted_iota(jnp.int32, (tq, tk), 0)
    col = lax.broadcasted_iota(jnp.int32, (tq, tk), 1)

    def step(kv, masked):
        start = pl.multiple_of(kv * tk, tk)
        for e in range(2):
            lanes = slice(e * LANES, (e + 1) * LANES)
            s = _dot_nt(q_ref[0, :, lanes], k_ref[0, pl.ds(start, tk), lanes])
            if masked:
                s = jnp.where(col <= row, s, NEG_INF)
            m_prev = ms[e][...]
            m_new = jnp.maximum(m_prev, jnp.max(s, axis=-1, keepdims=True))
            alpha = jnp.exp2(m_prev - m_new)
            pr = jnp.concatenate([jnp.exp2(s[:, c * LANES:(c + 1) * LANES] - m_new)
                                  for c in range(tk // LANES)], axis=1).astype(BF16)
            accs[e][...] = alpha * accs[e][...] + _dot(pr, v_ref[0, pl.ds(start, tk), lanes])
            ms[e][...] = m_new

    def body(kv, c):
        step(kv, False)
        return c

    lax.fori_loop(0, qi * (tq // tk), body, 0)
    for t in range(tq // tk):
        step(qi * (tq // tk) + t, True)
    low = lax.broadcasted_iota(jnp.int32, (tq, LANES), 1) < MLA_V
    r0 = a0[...] / pltpu.roll(a0[...], MLA_V, 1)
    r1 = a1[...] / pltpu.roll(a1[...], MLA_V, 1)
    o_ref[0] = jnp.where(low, r0, pltpu.roll(r1, MLA_V, 1)).astype(BF16)


def _mla_attention(qm, km, vm):
    B, S, _ = qm.shape
    tq = min(MLA_TQ, S)
    tk = tq
    pairs = MLA_HEADS // 2
    return pl.pallas_call(
        functools.partial(_mla_kernel, tq=tq, tk=tk),
        grid=(B, pairs, S // tq),
        in_specs=[pl.BlockSpec((1, tq, 2 * LANES), lambda b, p, i: (b, i, p)),
                  pl.BlockSpec((1, S, 2 * LANES), lambda b, p, i: (b, 0, p)),
                  pl.BlockSpec((1, S, 2 * LANES), lambda b, p, i: (b, 0, p))],
        out_specs=pl.BlockSpec((1, tq, LANES), lambda b, p, i: (b, i, p)),
        out_shape=jax.ShapeDtypeStruct((B, S, MIX_B), BF16),
        scratch_shapes=[pltpu.VMEM((tq, LANES), F32)] * 4,
        compiler_params=_params("parallel", "parallel", "arbitrary"),
        name="mla_flash",
    )(qm, km, vm)


def _route_tail(x1, gm_ref, wrh_ref, wrl_ref, br_ref, x1_ref, h_ref, ri_ref, rg_ref, cnt_ref, cnt_sc):
    x1_ref[...] = x1
    hn = _rms(x1, gm_ref[...])
    h_hi = hn.astype(BF16)
    h_ref[...] = h_hi
    h_lo = (hn - h_hi.astype(F32)).astype(BF16)
    logits = (_dot(h_hi, wrh_ref[...]) + _dot(h_hi, wrl_ref[...]) + _dot(h_lo, wrh_ref[...])
              + br_ref[...])
    tm = logits.shape[0]
    lane = lax.broadcasted_iota(jnp.int32, (tm, LANES), 1)
    lanef = lane.astype(F32)
    big = float(LANES)
    is_g = (lane >= N_EXPERTS) & (lane < N_EXPERTS + N_GROUPS)
    gl = jnp.where(is_g, logits, NEG_INF)
    gmax = jnp.max(gl, axis=-1, keepdims=True)
    gsum = jnp.sum(jnp.exp(gl - gmax), axis=-1, keepdims=True)
    g_val = 1.0 / gsum
    g_idx = jnp.min(jnp.where(gl == gmax, lanef, big), axis=-1, keepdims=True) - float(N_EXPERTS)
    in_grp = (lane < N_EXPERTS) & ((lane // EXPERTS_PER_GROUP).astype(F32) == g_idx)
    el = jnp.where(in_grp, logits, NEG_INF)
    e1 = jnp.max(el, axis=-1, keepdims=True)
    i1 = jnp.min(jnp.where(el == e1, lanef, big), axis=-1, keepdims=True)
    el2 = jnp.where(lanef == i1, NEG_INF, el)
    e2 = jnp.max(el2, axis=-1, keepdims=True)
    i2 = jnp.min(jnp.where(el2 == e2, lanef, big), axis=-1, keepdims=True)
    t = jnp.exp(e2 - e1)
    p1 = 1.0 / (1.0 + t)
    rg_ref[...] = jnp.where(lane == 0, p1 * g_val, jnp.where(lane == 1, t * p1 * g_val, 0.0))
    @pl.when(pl.program_id(0) == 0)
    def _():
        cnt_sc[...] = jnp.zeros(cnt_sc.shape, F32)

    oh1 = lanef == i1
    oh2 = lanef == i2
    oh = jnp.where(oh1 | oh2, 1.0, 0.0)
    rr = lax.broadcasted_iota(jnp.int32, (tm, tm), 0)
    cc = lax.broadcasted_iota(jnp.int32, (tm, tm), 1)
    strict = jnp.where(cc < rr, 1.0, 0.0).astype(BF16)
    before = _dot(strict, oh.astype(BF16)) + cnt_sc[...]
    rank1 = jnp.sum(jnp.where(oh1, before, 0.0), axis=-1, keepdims=True)
    rank2 = jnp.sum(jnp.where(oh2, before, 0.0), axis=-1, keepdims=True)
    cnt_sc[...] = cnt_sc[...] + jnp.sum(oh, axis=0, keepdims=True)
    cnt_ref[...] = jnp.broadcast_to(cnt_sc[...], cnt_ref.shape).astype(jnp.int32)
    ri_ref[...] = jnp.where(lane == 0, i1, jnp.where(lane == 1, i2, jnp.where(
        lane == 2, rank1, jnp.where(lane == 3, rank2, 0.0)))).astype(jnp.int32)


def _tail_specs(tm):
    row = lambda n: pl.BlockSpec((tm, n), lambda i: (i, 0))
    in_specs = [_full((1, D_MODEL)), _full((D_MODEL, LANES)), _full((D_MODEL, LANES)), _full((1, LANES))]
    out_specs = [row(D_MODEL), row(D_MODEL), row(LANES), row(LANES), _full((8, LANES))]
    return in_specs, out_specs


def _tail_shapes(T):
    return [jax.ShapeDtypeStruct((T, D_MODEL), F32), jax.ShapeDtypeStruct((T, D_MODEL), BF16),
            jax.ShapeDtypeStruct((T, LANES), jnp.int32), jax.ShapeDtypeStruct((T, LANES), F32),
            jax.ShapeDtypeStruct((8, LANES), jnp.int32)]


def _attn_post_kernel(oa_ref, ob_ref, x_ref, wo_ref, *tail_refs):
    y = _dot(oa_ref[...], wo_ref[0:MIX_A, :]) + _dot(ob_ref[...], wo_ref[MIX_A:, :])
    _route_tail(x_ref[...] + y, *tail_refs)


def _attn_post(oa, ob, x2, wo, tail):
    T = x2.shape[0]
    tm = min(ROW_TILE, T)
    row = lambda n: pl.BlockSpec((tm, n), lambda i: (i, 0))
    t_in, t_out = _tail_specs(tm)
    return pl.pallas_call(
        _attn_post_kernel,
        grid=(T // tm,),
        in_specs=[row(MIX_A), row(MIX_B), row(D_MODEL), _full(wo.shape)] + t_in,
        out_specs=t_out,
        out_shape=_tail_shapes(T),
        scratch_shapes=[pltpu.VMEM((1, LANES), F32)],
        compiler_params=_params("arbitrary"),
        name="attn_post",
    )(oa, ob, x2, wo, *tail)


def _expert_kernel(be_ref, nv_ref, xs_ref, wg_ref, wu_ref, wd_ref, y_ref, wg_bf, wu_bf, wd_bf):
    i = pl.program_id(0)

    @pl.when((i == 0) | (be_ref[i] != be_ref[jnp.maximum(i - 1, 0)]))
    def _():
        wg_bf[...] = wg_ref[0].astype(BF16)
        wu_bf[...] = wu_ref[0].astype(BF16)
        wd_bf[...] = wd_ref[0].astype(BF16)

    @pl.when(i < nv_ref[0])
    def _():
        xb = xs_ref[...]
        g = _dot(xb, wg_bf[...])
        u = _dot(xb, wu_bf[...])
        hid = (g * _sigmoid(g) * u).astype(BF16)
        y_ref[...] = _dot(hid, wd_bf[...]).astype(y_ref.dtype)

    @pl.when(i >= nv_ref[0])
    def _():
        y_ref[...] = jnp.zeros(y_ref.shape, y_ref.dtype)


def _expert_blocks(xs, block_expert, n_valid, layer, wg, wu, wd):
    rows = xs.shape[0]
    bm = MOE_ROWS
    last = lambda i, be, nv: jnp.minimum(i, nv[0] - 1)
    grid_spec = pltpu.PrefetchScalarGridSpec(
        num_scalar_prefetch=2,
        grid=(rows // bm,),
        in_specs=[pl.BlockSpec((bm, D_MODEL), lambda i, be, nv: (last(i, be, nv), 0)),
                  pl.BlockSpec((None, 1, D_MODEL, D_EXPERT), lambda i, be, nv: (layer, be[i], 0, 0)),
                  pl.BlockSpec((None, 1, D_MODEL, D_EXPERT), lambda i, be, nv: (layer, be[i], 0, 0)),
                  pl.BlockSpec((None, 1, D_EXPERT, D_MODEL), lambda i, be, nv: (layer, be[i], 0, 0))],
        out_specs=pl.BlockSpec((bm, D_MODEL), lambda i, be, nv: (i, 0)),
        scratch_shapes=[pltpu.VMEM((D_MODEL, D_EXPERT), BF16), pltpu.VMEM((D_MODEL, D_EXPERT), BF16),
                        pltpu.VMEM((D_EXPERT, D_MODEL), BF16)],
    )
    return pl.pallas_call(
        _expert_kernel,
        grid_spec=grid_spec,
        out_shape=jax.ShapeDtypeStruct((rows, D_MODEL), BF16),
        compiler_params=_params("arbitrary"),
        name="moe_experts",
    )(block_expert, n_valid, xs, wg, wu, wd)


def _combine_kernel(x_ref, r0_ref, r1_ref, rg_ref, gf_ref, o_ref, *, final):
    rg = rg_ref[...]
    y = x_ref[...] + rg[:, 0:1] * r0_ref[...].astype(F32) + rg[:, 1:2] * r1_ref[...].astype(F32)
    o_ref[...] = _rms(y, gf_ref[...]) if final else y


def _moe_combine(x1, r0, r1, rg, gfinal, final):
    T = x1.shape[0]
    tm = min(ROW_TILE, T)
    row = lambda n: pl.BlockSpec((tm, n), lambda i: (i, 0))
    return pl.pallas_call(
        functools.partial(_combine_kernel, final=final),
        grid=(T // tm,),
        in_specs=[row(D_MODEL), row(D_MODEL), row(D_MODEL), row(LANES), _full((1, D_MODEL))],
        out_specs=row(D_MODEL),
        out_shape=jax.ShapeDtypeStruct((T, D_MODEL), F32),
        compiler_params=_params("parallel"),
        name="moe_combine",
    )(x1, r0, r1, rg, gfinal)


def _moe(x1, h, ri, rg, cnt, layer, wg, wu, wd, gfinal, final):
    T = x1.shape[0]
    bm = MOE_ROWS
    experts = jnp.arange(N_EXPERTS, dtype=jnp.int32)
    counts = cnt[0, :N_EXPERTS]
    padded = (counts + bm - 1) // bm * bm
    pad_ends = jnp.cumsum(padded)
    dest2 = ri[:, 2:4] + jnp.sum(jnp.where(ri[:, 0:2, None] > experts, padded, 0), axis=-1)
    n_blocks = (2 * T) // bm + N_EXPERTS
    rows = n_blocks * bm
    tok = jnp.arange(2 * T, dtype=jnp.int32) // 2
    row_tok = jnp.zeros((rows,), jnp.int32).at[dest2.reshape(2 * T)].set(tok)
    block_start = jnp.arange(n_blocks, dtype=jnp.int32) * bm
    block_expert = jnp.minimum(jnp.sum((pad_ends[None, :] <= block_start[:, None]).astype(jnp.int32), axis=1),
                               N_EXPERTS - 1)
    n_valid = (pad_ends[-1:] // bm).astype(jnp.int32)
    xs = jnp.take(h, row_tok, axis=0)
    y_rows = _expert_blocks(xs, block_expert, n_valid, layer, wg, wu, wd)
    r0 = jnp.take(y_rows, dest2[:, 0], axis=0)
    r1 = jnp.take(y_rows, dest2[:, 1], axis=0)
    return _moe_combine(x1, r0, r1, rg, gfinal, final)


def _ssd_pre_kernel(x_ref, g_ref, w_ref, z_ref, xbc_ref, dt_ref):
    xn = _rms(x_ref[...], g_ref[...]).astype(BF16)
    step = 512
    for c in range(0, D_INNER, step):
        z_ref[:, c:c + step] = _dot(xn, w_ref[:, c:c + step]).astype(BF16)
    for c in range(0, XBC_DIM, step):
        xbc_ref[:, c:c + step] = _dot(xn, w_ref[:, D_INNER + c:D_INNER + c + step]).astype(BF16)
    dt_ref[...] = _dot(xn, w_ref[:, D_INNER + XBC_DIM:])


def _ssd_pre(x2, g, w_cat):
    T = x2.shape[0]
    tm = min(ROW_TILE, T)
    row = lambda n: pl.BlockSpec((tm, n), lambda i: (i, 0))
    return pl.pallas_call(
        _ssd_pre_kernel,
        grid=(T // tm,),
        in_specs=[row(D_MODEL), _full((1, D_MODEL)), _full(w_cat.shape)],
        out_specs=[row(D_INNER), row(XBC_DIM), row(LANES)],
        out_shape=[jax.ShapeDtypeStruct((T, D_INNER), BF16), jax.ShapeDtypeStruct((T, XBC_DIM), BF16),
                   jax.ShapeDtypeStruct((T, LANES), F32)],
        compiler_params=_params("parallel"),
        name="ssd_pre",
    )(x2, g, w_cat)


def _pair_expand(v, j, low):
    return jnp.where(low, v[:, 2 * j:2 * j + 1], v[:, 2 * j + 1:2 * j + 2])


def _ssd_scan_kernel(xbc_ref, dt_ref, cw_ref, cb_ref, dtb_ref, alog_ref, dsk_ref, y_ref,
                     ubuf, abuf, state):
    Q = CHUNK
    c = pl.program_id(1)

    @pl.when(c == 0)
    def _():
        ubuf[0:8, :] = jnp.zeros((8, XBC_DIM), F32)
        state[...] = jnp.zeros(state.shape, F32)

    ubuf[8:8 + Q, :] = xbc_ref[0].astype(F32)
    step = 512
    for cc in range(0, XBC_DIM, step):
        sl = slice(cc, cc + step)
        acc = cb_ref[:, sl] + cw_ref[3:4, sl] * ubuf[8:8 + Q, sl]
        for kk in range(D_CONV - 1):
            acc = acc + cw_ref[kk:kk + 1, sl] * ubuf[5 + kk:5 + kk + Q, sl]
        abuf[:, sl] = acc * _sigmoid(acc)
    ubuf[0:8, :] = ubuf[Q:Q + 8, :]

    x_dt = dt_ref[0] + dtb_ref[...]
    dt = jnp.maximum(x_dt, 0.0) + jnp.log(1.0 + jnp.exp(-jnp.abs(x_dt)))
    da = dt * (-jnp.exp(alog_ref[...]))
    ri = lax.broadcasted_iota(jnp.int32, (Q, Q), 0)
    ci = lax.broadcasted_iota(jnp.int32, (Q, Q), 1)
    causal = ci <= ri
    tril = jnp.where(causal, 1.0, 0.0).astype(BF16)
    d_hi = da.astype(BF16)
    r1 = da - d_hi.astype(F32)
    d_mid = r1.astype(BF16)
    d_lo = (r1 - d_mid.astype(F32)).astype(BF16)
    cs = _dot(tril, d_hi) + _dot(tril, d_mid) + _dot(tril, d_lo)
    cs_t = jnp.transpose(cs)
    cs_last = cs[Q - 1:Q, :]
    ecs = jnp.exp(cs)
    to_end = jnp.exp(cs_last - cs)
    dec_last = jnp.exp(cs_last)
    low = lax.broadcasted_iota(jnp.int32, (Q, LANES), 1) < SSD_HEADDIM
    low1 = low[0:1, :]
    hg = SSD_HEADS // SSD_GROUPS
    gw = hg * SSD_HEADDIM
    for g in range(SSD_GROUPS):
        b_g = abuf[:, D_INNER + g * D_STATE:D_INNER + (g + 1) * D_STATE]
        c_g = abuf[:, D_INNER + SSD_GROUPS * D_STATE + g * D_STATE:
                   D_INNER + SSD_GROUPS * D_STATE + (g + 1) * D_STATE].astype(BF16)
        cb = _dot_nt(c_g, b_g.astype(BF16))
        b_t = jnp.transpose(b_g).astype(BF16)
        st = state[g]
        y_inter = _dot(c_g, st.astype(BF16))
        xw_parts, dl_parts = [], []
        for p in range(hg // 2):
            j = g * (hg // 2) + p
            lanes = slice(j * LANES, (j + 1) * LANES)
            x_p = abuf[:, lanes]
            xdt = x_p * _pair_expand(dt, j, low)
            ys = []
            for e in range(2):
                h = 2 * j + e
                seg = cs[:, h:h + 1] - cs_t[h:h + 1, :]
                lmat = jnp.exp(jnp.where(causal, seg, NEG_INF))
                ys.append(_dot((cb * lmat).astype(BF16), xdt.astype(BF16)))
            y_p = (jnp.where(low, ys[0], ys[1])
                   + y_inter[:, p * LANES:(p + 1) * LANES] * _pair_expand(ecs, j, low)
                   + dsk_ref[:, lanes] * x_p)
            y_ref[0, :, lanes] = y_p.astype(y_ref.dtype)
            xw_parts.append((xdt * _pair_expand(to_end, j, low)).astype(BF16))
            dl_parts.append(_pair_expand(dec_last, j, low1))
        xw = jnp.concatenate(xw_parts, axis=1)
        dl = jnp.concatenate(dl_parts, axis=1)
        state[g] = st * dl + _dot(b_t, xw)
    del gw


def _ssd_scan(xbc, dt, conv_w, conv_b, dt_bias, a_log, d_skip):
    B, S, _ = xbc.shape
    Q = CHUNK
    hg = SSD_HEADS // SSD_GROUPS
    return pl.pallas_call(
        _ssd_scan_kernel,
        grid=(B, S // Q),
        in_specs=[pl.BlockSpec((1, Q, XBC_DIM), lambda b, c: (b, c, 0)),
                  pl.BlockSpec((1, Q, LANES), lambda b, c: (b, c, 0)),
                  _full((D_CONV, XBC_DIM)), _full((1, XBC_DIM)), _full((1, LANES)), _full((1, LANES)),
                  _full((1, D_INNER))],
        out_specs=pl.BlockSpec((1, Q, D_INNER), lambda b, c: (b, c, 0)),
        out_shape=jax.ShapeDtypeStruct((B, S, D_INNER), BF16),
        scratch_shapes=[pltpu.VMEM((Q + 8, XBC_DIM), F32), pltpu.VMEM((Q, XBC_DIM), F32),
                        pltpu.VMEM((SSD_GROUPS, D_STATE, hg * SSD_HEADDIM), F32)],
        compiler_params=_params("parallel", "arbitrary"),
        name="ssd_scan",
    )(xbc, dt, conv_w, conv_b, dt_bias, a_log, d_skip)


def _ssd_post_kernel(y_ref, z_ref, gn_ref, x_ref, wo_ref, *tail_refs):
    gsz = D_INNER // SSD_GROUPS
    out = None
    for g in range(SSD_GROUPS):
        sl = slice(g * gsz, (g + 1) * gsz)
        z = z_ref[:, sl].astype(F32)
        yz = y_ref[:, sl].astype(F32) * (z * _sigmoid(z))
        part = _dot(_rms(yz, gn_ref[:, sl]).astype(BF16), wo_ref[sl, :])
        out = part if out is None else out + part
    _route_tail(x_ref[...] + out, *tail_refs)


def _ssd_post(y, z, gn, x2, wo, tail):
    T = x2.shape[0]
    tm = min(ROW_TILE, T)
    row = lambda n: pl.BlockSpec((tm, n), lambda i: (i, 0))
    t_in, t_out = _tail_specs(tm)
    return pl.pallas_call(
        _ssd_post_kernel,
        grid=(T // tm,),
        in_specs=[row(D_INNER), row(D_INNER), _full((1, D_INNER)), row(D_MODEL), _full(wo.shape)] + t_in,
        out_specs=t_out,
        out_shape=_tail_shapes(T),
        scratch_shapes=[pltpu.VMEM((1, LANES), F32)],
        compiler_params=_params("arbitrary"),
        name="ssd_post",
    )(y, z, gn, x2, wo, *tail)


def _rope_tables(positions):
    pos = positions.reshape(-1).astype(F32)[:, None]
    T = pos.shape[0]

    def tables(rot, lead, period):
        half = rot // 2
        inv = ROPE_THETA ** (-jnp.arange(0, rot, 2, dtype=F32) / rot)
        ang = pos * inv
        cos, sin = jnp.cos(ang), jnp.sin(ang)
        one = lambda n: jnp.ones((T, n), F32)
        zero = lambda n: jnp.zeros((T, n), F32)
        tail = period - lead - rot
        c = jnp.concatenate([one(lead), cos, cos, one(tail)], axis=1)
        s1 = jnp.concatenate([zero(lead), -sin, zero(half), zero(tail)], axis=1)
        s2 = jnp.concatenate([zero(lead), zero(half), sin, zero(tail)], axis=1)
        rep = LANES // period
        return [jnp.tile(t, (1, rep)) for t in (c, s1, s2)]

    return tables(ROPE_DIM_A, 0, HEAD_DIM_A) + tables(MLA_ROPE, MLA_NOPE, LANES)


def _pad_cols(w, n):
    return jnp.pad(w, ((0, 0), (0, n - w.shape[1])))


def _head_pad(w, width):
    K = w.shape[0]
    w = w.reshape(K, -1, width)
    return jnp.pad(w, ((0, 0), (0, 0), (0, LANES - width))).reshape(K, -1)


def _router_weights(w_group, b_group, w_router, b_router):
    w = _pad_cols(jnp.concatenate([w_router, w_group], axis=1), LANES)
    b = _pad_cols(jnp.concatenate([b_router, b_group])[None, :], LANES)
    hi = w.astype(BF16)
    lo = (w - hi.astype(F32)).astype(BF16)
    return hi, lo, b


def kernel(x, positions, attn_norm, w_in_attn, mla_q_norm, w_uq, mla_kv_norm, w_ukv, w_out_attn,
           ssd_norm, w_in_ssd, conv_w, conv_b, dt_bias, a_log, d_skip, gate_norm, w_out_ssd,
           moe_norm, w_group, b_group, w_router, b_router, w_gate, w_up, w_down, final_norm):
    B, S, D = x.shape
    T = B * S
    x2 = x.reshape(T, D)
    tabs = _rope_tables(positions)

    def tail_params(layer):
        hi, lo, b = _router_weights(w_group[layer], b_group[layer], w_router[layer], b_router[layer])
        return [moe_norm[layer][None, :], hi, lo, b]

    w_in = w_in_attn[0]
    kr_off = 3 * MIX_A + MLA_Q_RANK + MLA_KV_RANK
    kr_cols = jnp.pad(w_in[:, kr_off:], ((0, 0), (MLA_NOPE, LANES - MLA_NOPE - MLA_ROPE)))
    w_cat = jnp.concatenate([w_in[:, :kr_off], kr_cols], axis=1).astype(BF16)
    wuq = _head_pad(w_uq[0], MLA_NOPE + MLA_ROPE).astype(BF16)
    ukv = w_ukv[0].reshape(MLA_KV_RANK, MLA_HEADS, MLA_NOPE + MLA_V)
    wuk = _head_pad(ukv[:, :, :MLA_NOPE].reshape(MLA_KV_RANK, -1), MLA_NOPE).astype(BF16)
    wuv = _head_pad(ukv[:, :, MLA_NOPE:].reshape(MLA_KV_RANK, -1), MLA_V).astype(BF16)
    qa, ka, va, qm, km, vm = _attn_pre(x2, attn_norm[0][None, :], w_cat, mla_q_norm[0][None, :], wuq,
                                       mla_kv_norm[0][None, :], wuk, wuv, tabs)
    shp = lambda t: t.reshape(B, S, t.shape[-1])
    prev = None
    for d in DILATIONS:
        prev = _dilated_branch(shp(qa), shp(ka), shp(va), d, prev)
    oa = prev[0].reshape(T, MIX_A)
    ob = _mla_attention(shp(qm), shp(km), shp(vm)).reshape(T, MIX_B)
    x1, h, ri, rg, cnt = _attn_post(oa, ob, x2, w_out_attn[0].astype(BF16), tail_params(0))
    x2 = _moe(x1, h, ri, rg, cnt, 0, w_gate, w_up, w_down, final_norm[None, :], False)

    w_cat = _pad_cols(w_in_ssd[0], D_INNER + XBC_DIM + LANES).astype(BF16)
    z, xbc, dt = _ssd_pre(x2, ssd_norm[0][None, :], w_cat)
    y = _ssd_scan(xbc.reshape(B, S, XBC_DIM), dt.reshape(B, S, LANES), conv_w[0], conv_b[0][None, :],
                  _pad_cols(dt_bias[0][None, :], LANES), _pad_cols(a_log[0][None, :], LANES),
                  jnp.repeat(d_skip[0], SSD_HEADDIM)[None, :])
    x1, h, ri, rg, cnt = _ssd_post(y.reshape(T, D_INNER), z, gate_norm[0][None, :], x2,
                                   w_out_ssd[0].astype(BF16), tail_params(1))
    out = _moe(x1, h, ri, rg, cnt, 1, w_gate, w_up, w_down, final_norm[None, :], True)
    return out.reshape(B, S, D)
```

```python
import functools

import jax
import jax.numpy as jnp
from jax import lax
from jax.experimental import pallas as pl
from jax.experimental.pallas import tpu as pltpu
from jax.experimental.pallas import tpu_sc as plsc

F32 = jnp.float32
BF16 = jnp.bfloat16

D_MODEL = 1024
HEADS_A = 8
HEAD_DIM_A = 64
ROPE_DIM_A = 16
DILATIONS = (1, 4, 16)
SPAN = 128
MLA_HEADS = 8
MLA_NOPE = 64
MLA_ROPE = 32
MLA_V = 64
MLA_Q_RANK = 256
MLA_KV_RANK = 128
ROPE_THETA = 500000.0
MIX_A = HEADS_A * HEAD_DIM_A
MIX_B = MLA_HEADS * MLA_V
D_INNER = 2048
SSD_HEADDIM = 64
SSD_HEADS = 32
SSD_GROUPS = 4
D_STATE = 128
D_CONV = 4
CHUNK = 256
XBC_DIM = D_INNER + 2 * SSD_GROUPS * D_STATE
N_GROUPS = 4
EXPERTS_PER_GROUP = 8
N_EXPERTS = 32
D_EXPERT = 512
NORM_EPS = 1e-6

LANES = 128
VMEM_LIMIT = 56 * 1024 * 1024
ROW_TILE = 512
MOE_ROWS = 256
SC_CORES = 2
SC_WORKERS = 32
SC_CHUNK = 64
MLA_TQ = 512
MLA_TK = 512
NEG_INF = float("-inf")
LOG2E = 1.4426950408889634


def _params(*sem):
    return pltpu.CompilerParams(dimension_semantics=sem, vmem_limit_bytes=VMEM_LIMIT)


def _dot(a, b):
    return jnp.dot(a, b, preferred_element_type=F32)


def _dot_nt(a, b):
    return lax.dot_general(a, b, (((1,), (1,)), ((), ())), preferred_element_type=F32)


def _rms(x, g):
    return x * lax.rsqrt(jnp.mean(x * x, axis=-1, keepdims=True) + NORM_EPS) * g


def _sigmoid(x):
    return 1.0 / (1.0 + jnp.exp(-x))


def _rope(t, c, s, half, first):
    partner = jnp.where(first, pltpu.roll(t, LANES - half, 1), pltpu.roll(t, half, 1))
    return t * c + partner * s


def _pack_bf16_pairs(x):
    n = x.shape[1] // 2
    bits = lax.bitcast_convert_type(x.astype(BF16).astype(F32), jnp.uint32)
    return (bits[:, :n] >> 16) | (bits[:, n:] & jnp.uint32(0xFFFF0000))


def _unpack_bf16_pairs(w):
    lo = lax.bitcast_convert_type(w << 16, F32)
    hi = lax.bitcast_convert_type(w & jnp.uint32(0xFFFF0000), F32)
    return jnp.concatenate([lo, hi], axis=1)


def _full(shape):
    return pl.BlockSpec(shape, lambda *_: (0,) * len(shape))


def _attn_pre_kernel(x_ref, g_ref, w_ref, qn_ref, wuq_ref, kvn_ref, wuk_ref, wuv_ref,
                     ca_ref, sa_ref, cb_ref, sb_ref,
                     qa_ref, ka_ref, va_ref, qm_ref, km_ref, vm_ref):
    xn = _rms(x_ref[...], g_ref[...]).astype(BF16)
    ca, sa, cb, sb = ca_ref[...], sa_ref[...], cb_ref[...], sb_ref[...]
    half_a = ROPE_DIM_A // 2
    half_b = MLA_ROPE // 2
    lane = lax.broadcasted_iota(jnp.int32, ca.shape, 1)
    first_a = lane % HEAD_DIM_A < half_a
    first_b = (lane >= MLA_NOPE) & (lane < MLA_NOPE + half_b)

    q = _dot(xn, w_ref[:, 0:MIX_A]) * (HEAD_DIM_A ** -0.5)
    k = _dot(xn, w_ref[:, MIX_A:2 * MIX_A])
    for c in range(MIX_A // LANES):
        sl = slice(c * LANES, (c + 1) * LANES)
        qa_ref[:, sl] = _rope(q[:, sl], ca, sa, half_a, first_a).astype(BF16)
        ka_ref[:, sl] = _rope(k[:, sl], ca, sa, half_a, first_a).astype(BF16)
    va_ref[...] = _dot(xn, w_ref[:, 2 * MIX_A:3 * MIX_A]).astype(BF16)

    o = 3 * MIX_A
    cq = _dot(xn, w_ref[:, o:o + MLA_Q_RANK])
    ckv = _dot(xn, w_ref[:, o + MLA_Q_RANK:o + MLA_Q_RANK + MLA_KV_RANK])
    kr = _dot(xn, w_ref[:, o + MLA_Q_RANK + MLA_KV_RANK:o + MLA_Q_RANK + MLA_KV_RANK + LANES])
    cqn = _rms(cq, qn_ref[...]).astype(BF16)
    kvn = _rms(ckv, kvn_ref[...]).astype(BF16)
    krr = _rope(kr, cb, sb, half_b, first_b)
    qm = _dot(cqn, wuq_ref[...]) * ((MLA_NOPE + MLA_ROPE) ** -0.5 * LOG2E)
    km = _dot(kvn, wuk_ref[...])
    for h in range(MLA_HEADS):
        sl = slice(h * LANES, (h + 1) * LANES)
        qm_ref[:, sl] = _rope(qm[:, sl], cb, sb, half_b, first_b).astype(BF16)
        km_ref[:, sl] = (km[:, sl] + krr).astype(BF16)
    vm = _dot(kvn, wuv_ref[...])
    low = lax.broadcasted_iota(jnp.int32, vm.shape, 1) % LANES < MLA_V
    vm_ref[...] = jnp.where(low, vm, 1.0).astype(BF16)


def _attn_pre(x2, g, w_cat, qn, wuq, kvn, wuk, wuv, tabs):
    T = x2.shape[0]
    tm = min(ROW_TILE, T)
    row = lambda n: pl.BlockSpec((tm, n), lambda i: (i, 0))
    wide = MLA_HEADS * LANES
    return pl.pallas_call(
        _attn_pre_kernel,
        grid=(T // tm,),
        in_specs=[row(D_MODEL), _full((1, D_MODEL)), _full(w_cat.shape), _full((1, MLA_Q_RANK)),
                  _full(wuq.shape), _full((1, MLA_KV_RANK)), _full(wuk.shape), _full(wuv.shape)]
                 + [row(LANES)] * 4,
        out_specs=[row(MIX_A), row(MIX_A), row(MIX_A), row(wide), row(wide), row(wide)],
        out_shape=[jax.ShapeDtypeStruct((T, MIX_A), BF16)] * 3
                  + [jax.ShapeDtypeStruct((T, wide), BF16)] * 3,
        compiler_params=_params("parallel"),
        name="attn_pre",
    )(x2, g, w_cat, qn, wuq, kvn, wuk, wuv, *tabs)


def _dilated_kernel(*refs, tq, carry):
    if carry:
        q_ref, kc_ref, kp_ref, vc_ref, vp_ref, op_ref, lp_ref, o_ref, lse_ref, kbuf, vbuf = refs
    else:
        q_ref, kc_ref, kp_ref, vc_ref, vp_ref, o_ref, lse_ref, kbuf, vbuf = refs
    j = pl.program_id(2)
    kbuf[0:SPAN, :] = kp_ref[0]
    kbuf[SPAN:, :] = kc_ref[0]
    vbuf[0:SPAN, :] = vp_ref[0]
    vbuf[SPAN:, :] = vc_ref[0]
    low = lax.broadcasted_iota(jnp.int32, (SPAN, LANES), 1) < HEAD_DIM_A
    qi = lax.broadcasted_iota(jnp.int32, (SPAN, 2 * SPAN), 0)
    ki = lax.broadcasted_iota(jnp.int32, (SPAN, 2 * SPAN), 1)
    band = (ki >= qi) & (ki <= qi + SPAN)
    first_valid = band & ((ki >= SPAN) | (j > 0))
    for sub in range(tq // SPAN):
        valid = first_valid if sub == 0 else band
        rows = slice(sub * SPAN, (sub + 1) * SPAN)
        krows = slice(sub * SPAN, (sub + 2) * SPAN)
        for p in range(MIX_A // LANES):
            lanes = slice(p * LANES, (p + 1) * LANES)
            q2 = q_ref[0, rows, lanes]
            kk = kbuf[krows, lanes]
            vv = vbuf[krows, lanes]
            os_, ls_ = [], []
            for e in range(2):
                qe = jnp.where(low if e == 0 else jnp.logical_not(low), q2, jnp.zeros_like(q2))
                s = jnp.where(valid, _dot_nt(qe, kk), NEG_INF)
                m = jnp.max(s, axis=-1, keepdims=True)
                pr = jnp.exp(s - m)
                l = jnp.sum(pr, axis=-1, keepdims=True)
                os_.append(_dot(pr.astype(BF16), vv) / l)
                ls_.append(m + jnp.log(l))
            o2 = jnp.where(low, os_[0], os_[1])
            l2 = jnp.where(low, ls_[0], ls_[1])
            if carry:
                lp = lp_ref[0, rows, lanes]
                mx = jnp.maximum(lp, l2)
                wp = jnp.exp(lp - mx)
                wc = jnp.exp(l2 - mx)
                den = wp + wc
                o2 = (wp * op_ref[0, rows, lanes].astype(F32) + wc * o2) / den
                l2 = mx + jnp.log(den)
            o_ref[0, rows, lanes] = o2.astype(BF16)
            lse_ref[0, rows, lanes] = l2


def _dilated_branch(q, k, v, d, prev):
    B, S, _ = q.shape
    L = S // d
    tq = min(512, L)
    view = lambda t: t.reshape(B, L, d * t.shape[-1])
    cur = pl.BlockSpec((1, tq, MIX_A), lambda b, r, j: (b, j, r))
    prv = pl.BlockSpec((1, SPAN, MIX_A), lambda b, r, j: (b, jnp.maximum(j * (tq // SPAN) - 1, 0), r))
    carry = prev is not None
    ins = [view(q), view(k), view(k), view(v), view(v)]
    in_specs = [cur, cur, prv, cur, prv]
    if carry:
        ins += [view(prev[0]), view(prev[1])]
        in_specs += [cur, cur]
    o, lse = pl.pallas_call(
        functools.partial(_dilated_kernel, tq=tq, carry=carry),
        grid=(B, d, L // tq),
        in_specs=in_specs,
        out_specs=[cur, cur],
        out_shape=[jax.ShapeDtypeStruct((B, L, d * MIX_A), BF16),
                   jax.ShapeDtypeStruct((B, L, d * MIX_A), F32)],
        scratch_shapes=[pltpu.VMEM((tq + SPAN, MIX_A), BF16), pltpu.VMEM((tq + SPAN, MIX_A), BF16)],
        compiler_params=_params("parallel", "parallel", "parallel"),
        name=f"dilated_d{d}",
    )(*ins)
    return o.reshape(B, S, MIX_A), lse.reshape(B, S, MIX_A)


def _mla_kernel(q_ref, k_ref, v_ref, o_ref, m0, m1, a0, a1, *, tq, tk):
    qi = pl.program_id(2)
    ms, accs = (m0, m1), (a0, a1)
    for e in range(2):
        ms[e][...] = jnp.full(ms[e].shape, NEG_INF, F32)
        accs[e][...] = jnp.zeros(accs[e].shape, F32)
    row = lax.broadcasted_iota(jnp.int32, (tq, tk), 0)
    col = lax.broadcasted_iota(jnp.int32, (tq, tk), 1)

    def step(kv, masked):
        start = pl.multiple_of(kv * tk, tk)
        for e in range(2):
            lanes = slice(e * LANES, (e + 1) * LANES)
            s = _dot_nt(q_ref[0, :, lanes], k_ref[0, pl.ds(start, tk), lanes])
            if masked:
                s = jnp.where(col <= row, s, NEG_INF)
            m_prev = ms[e][...]
            m_new = jnp.maximum(m_prev, jnp.max(s, axis=-1, keepdims=True))
            alpha = jnp.exp2(m_prev - m_new)
            pr = jnp.concatenate([jnp.exp2(s[:, c * LANES:(c + 1) * LANES] - m_new)
                                  for c in range(tk // LANES)], axis=1).astype(BF16)
            accs[e][...] = alpha * accs[e][...] + _dot(pr, v_ref[0, pl.ds(start, tk), lanes])
            ms[e][...] = m_new

    def body(kv, c):
        step(kv, False)
        return c

    lax.fori_loop(0, qi * (tq // tk), body, 0)
    for t in range(tq // tk):
        step(qi * (tq // tk) + t, True)
    low = lax.broadcasted_iota(jnp.int32, (tq, LANES), 1) < MLA_V
    r0 = a0[...] / pltpu.roll(a0[...], MLA_V, 1)
    r1 = a1[...] / pltpu.roll(a1[...], MLA_V, 1)
    o_ref[0] = jnp.where(low, r0, pltpu.roll(r1, MLA_V, 1)).astype(BF16)


def _mla_attention(qm, km, vm):
    B, S, _ = qm.shape
    tq = min(MLA_TQ, S)
    tk = tq
    pairs = MLA_HEADS // 2
    return pl.pallas_call(
        functools.partial(_mla_kernel, tq=tq, tk=tk),
        grid=(B, pairs, S // tq),
        in_specs=[pl.BlockSpec((1, tq, 2 * LANES), lambda b, p, i: (b, i, p)),
                  pl.BlockSpec((1, S, 2 * LANES), lambda b, p, i: (b, 0, p)),
                  pl.BlockSpec((1, S, 2 * LANES), lambda b, p, i: (b, 0, p))],
        out_specs=pl.BlockSpec((1, tq, LANES), lambda b, p, i: (b, i, p)),
        out_shape=jax.ShapeDtypeStruct((B, S, MIX_B), BF16),
        scratch_shapes=[pltpu.VMEM((tq, LANES), F32)] * 4,
        compiler_params=_params("parallel", "parallel", "arbitrary"),
        name="mla_flash",
    )(qm, km, vm)


def _route_tail(x1, gm_ref, wrh_ref, wrl_ref, br_ref, x1_ref, h_ref, ri_ref, rg_ref, cnt_ref, cnt_sc):
    x1_ref[...] = x1
    hn = _rms(x1, gm_ref[...])
    h_hi = hn.astype(BF16)
    h_ref[...] = _pack_bf16_pairs(hn)
    h_lo = (hn - h_hi.astype(F32)).astype(BF16)
    logits = (_dot(h_hi, wrh_ref[...]) + _dot(h_hi, wrl_ref[...]) + _dot(h_lo, wrh_ref[...])
              + br_ref[...])
    tm = logits.shape[0]
    lane = lax.broadcasted_iota(jnp.int32, (tm, LANES), 1)
    lanef = lane.astype(F32)
    big = float(LANES)
    is_g = (lane >= N_EXPERTS) & (lane < N_EXPERTS + N_GROUPS)
    gl = jnp.where(is_g, logits, NEG_INF)
    gmax = jnp.max(gl, axis=-1, keepdims=True)
    gsum = jnp.sum(jnp.exp(gl - gmax), axis=-1, keepdims=True)
    g_val = 1.0 / gsum
    g_idx = jnp.min(jnp.where(gl == gmax, lanef, big), axis=-1, keepdims=True) - float(N_EXPERTS)
    in_grp = (lane < N_EXPERTS) & ((lane // EXPERTS_PER_GROUP).astype(F32) == g_idx)
    el = jnp.where(in_grp, logits, NEG_INF)
    e1 = jnp.max(el, axis=-1, keepdims=True)
    i1 = jnp.min(jnp.where(el == e1, lanef, big), axis=-1, keepdims=True)
    el2 = jnp.where(lanef == i1, NEG_INF, el)
    e2 = jnp.max(el2, axis=-1, keepdims=True)
    i2 = jnp.min(jnp.where(el2 == e2, lanef, big), axis=-1, keepdims=True)
    t = jnp.exp(e2 - e1)
    p1 = 1.0 / (1.0 + t)
    rg_ref[...] = jnp.where(lane == 0, p1 * g_val, jnp.where(lane == 1, t * p1 * g_val, 0.0))
    @pl.when(pl.program_id(0) == 0)
    def _():
        cnt_sc[...] = jnp.zeros(cnt_sc.shape, F32)

    oh1 = lanef == i1
    oh2 = lanef == i2
    oh = jnp.where(oh1 | oh2, 1.0, 0.0)
    rr = lax.broadcasted_iota(jnp.int32, (tm, tm), 0)
    cc = lax.broadcasted_iota(jnp.int32, (tm, tm), 1)
    strict = jnp.where(cc < rr, 1.0, 0.0).astype(BF16)
    before = _dot(strict, oh.astype(BF16)) + cnt_sc[...]
    rank1 = jnp.sum(jnp.where(oh1, before, 0.0), axis=-1, keepdims=True)
    rank2 = jnp.sum(jnp.where(oh2, before, 0.0), axis=-1, keepdims=True)
    cnt_sc[...] = cnt_sc[...] + jnp.sum(oh, axis=0, keepdims=True)
    cnt_ref[...] = jnp.broadcast_to(cnt_sc[...], cnt_ref.shape).astype(jnp.int32)
    ri_ref[...] = jnp.where(lane == 0, i1, jnp.where(lane == 1, i2, jnp.where(
        lane == 2, rank1, jnp.where(lane == 3, rank2, 0.0)))).astype(jnp.int32)


def _tail_specs(tm):
    row = lambda n: pl.BlockSpec((tm, n), lambda i: (i, 0))
    in_specs = [_full((1, D_MODEL)), _full((D_MODEL, LANES)), _full((D_MODEL, LANES)), _full((1, LANES))]
    out_specs = [row(D_MODEL), row(D_MODEL // 2), row(LANES), row(LANES), _full((8, LANES))]
    return in_specs, out_specs


def _tail_shapes(T):
    return [jax.ShapeDtypeStruct((T, D_MODEL), F32), jax.ShapeDtypeStruct((T, D_MODEL // 2), jnp.uint32),
            jax.ShapeDtypeStruct((T, LANES), jnp.int32), jax.ShapeDtypeStruct((T, LANES), F32),
            jax.ShapeDtypeStruct((8, LANES), jnp.int32)]


def _attn_post_kernel(oa_ref, ob_ref, x_ref, wo_ref, *tail_refs):
    y = _dot(oa_ref[...], wo_ref[0:MIX_A, :]) + _dot(ob_ref[...], wo_ref[MIX_A:, :])
    _route_tail(x_ref[...] + y, *tail_refs)


def _attn_post(oa, ob, x2, wo, tail):
    T = x2.shape[0]
    tm = min(ROW_TILE, T)
    row = lambda n: pl.BlockSpec((tm, n), lambda i: (i, 0))
    t_in, t_out = _tail_specs(tm)
    return pl.pallas_call(
        _attn_post_kernel,
        grid=(T // tm,),
        in_specs=[row(MIX_A), row(MIX_B), row(D_MODEL), _full(wo.shape)] + t_in,
        out_specs=t_out,
        out_shape=_tail_shapes(T),
        scratch_shapes=[pltpu.VMEM((1, LANES), F32)],
        compiler_params=_params("arbitrary"),
        name="attn_post",
    )(oa, ob, x2, wo, *tail)


def _expert_kernel(be_ref, nr_ref, xs_ref, wg_ref, wu_ref, wd_ref, y_ref, wg_bf, wu_bf, wd_bf):
    i = pl.program_id(0)

    @pl.when((i == 0) | (be_ref[i] != be_ref[jnp.maximum(i - 1, 0)]))
    def _():
        wg_bf[...] = wg_ref[0].astype(BF16)
        wu_bf[...] = wu_ref[0].astype(BF16)
        wd_bf[...] = wd_ref[0].astype(BF16)

    @pl.when(nr_ref[i] > 0)
    def _():
        live = lax.broadcasted_iota(jnp.int32, xs_ref.shape, 0) < nr_ref[i]
        xb = _unpack_bf16_pairs(jnp.where(live, xs_ref[...], jnp.uint32(0))).astype(BF16)
        g = _dot(xb, wg_bf[...])
        u = _dot(xb, wu_bf[...])
        hid = (g * _sigmoid(g) * u).astype(BF16)
        y_ref[...] = _pack_bf16_pairs(_dot(hid, wd_bf[...]))

    @pl.when(nr_ref[i] == 0)
    def _():
        y_ref[...] = jnp.zeros(y_ref.shape, y_ref.dtype)


def _expert_blocks(xs, block_expert, block_rows, layer, wg, wu, wd):
    rows = xs.shape[0]
    bm = MOE_ROWS
    half = D_MODEL // 2
    grid_spec = pltpu.PrefetchScalarGridSpec(
        num_scalar_prefetch=2,
        grid=(rows // bm,),
        in_specs=[pl.BlockSpec((bm, half), lambda i, be, nr: (i, 0)),
                  pl.BlockSpec((None, 1, D_MODEL, D_EXPERT), lambda i, be, nr: (layer, be[i], 0, 0)),
                  pl.BlockSpec((None, 1, D_MODEL, D_EXPERT), lambda i, be, nr: (layer, be[i], 0, 0)),
                  pl.BlockSpec((None, 1, D_EXPERT, D_MODEL), lambda i, be, nr: (layer, be[i], 0, 0))],
        out_specs=pl.BlockSpec((bm, half), lambda i, be, nr: (i, 0)),
        scratch_shapes=[pltpu.VMEM((D_MODEL, D_EXPERT), BF16), pltpu.VMEM((D_MODEL, D_EXPERT), BF16),
                        pltpu.VMEM((D_EXPERT, D_MODEL), BF16)],
    )
    return pl.pallas_call(
        _expert_kernel,
        grid_spec=grid_spec,
        out_shape=jax.ShapeDtypeStruct((rows, half), jnp.uint32),
        compiler_params=_params("arbitrary"),
        name="moe_experts",
    )(block_expert, block_rows, xs, wg, wu, wd)


def _combine_kernel(x_ref, r0_ref, r1_ref, rg_ref, gf_ref, o_ref, *, final):
    rg = rg_ref[...]
    y = (x_ref[...] + rg[:, 0:1] * _unpack_bf16_pairs(r0_ref[...])
         + rg[:, 1:2] * _unpack_bf16_pairs(r1_ref[...]))
    o_ref[...] = _rms(y, gf_ref[...]) if final else y


def _moe_combine(x1, yg, rg, gfinal, final):
    T = x1.shape[0]
    tm = min(ROW_TILE, T)
    half = D_MODEL // 2
    row = lambda n: pl.BlockSpec((tm, n), lambda i: (i, 0))
    return pl.pallas_call(
        functools.partial(_combine_kernel, final=final),
        grid=(T // tm,),
        in_specs=[row(D_MODEL), row(half), pl.BlockSpec((tm, half), lambda i: (i + T // tm, 0)),
                  row(LANES), _full((1, D_MODEL))],
        out_specs=row(D_MODEL),
        out_shape=jax.ShapeDtypeStruct((T, D_MODEL), F32),
        compiler_params=_params("parallel"),
        name="moe_combine",
    )(x1, yg, yg, rg, gfinal)


def _sc_rows(n_rows):
    per_worker = n_rows // SC_WORKERS
    assert n_rows % SC_WORKERS == 0 and per_worker % SC_CHUNK == 0
    return per_worker, per_worker // SC_CHUNK


def _sc_dispatch(hp, dest0, dest1, rows):
    T, W = hp.shape
    per_w, n_sub = _sc_rows(T)
    mesh = plsc.VectorSubcoreMesh(core_axis_name="c", subcore_axis_name="s")

    @functools.partial(
        pl.kernel, mesh=mesh,
        out_type=jax.ShapeDtypeStruct((rows, W), hp.dtype),
        scratch_types=[pltpu.VMEM((n_sub, SC_CHUNK), jnp.int32), pltpu.VMEM((n_sub, SC_CHUNK), jnp.int32),
                       pltpu.VMEM((SC_CHUNK, W), hp.dtype)],
    )
    def k(hp_hbm, d0_hbm, d1_hbm, out_hbm, d0_v, d1_v, buf):
        wid = lax.axis_index("s") * SC_CORES + lax.axis_index("c")
        pltpu.sync_copy(d0_hbm.at[wid], d0_v)
        pltpu.sync_copy(d1_hbm.at[wid], d1_v)

        @pl.loop(0, n_sub)
        def _(j):
            pltpu.sync_copy(hp_hbm.at[pl.ds(wid * per_w + j * SC_CHUNK, SC_CHUNK)], buf)
            pltpu.sync_copy(buf, out_hbm.at[d0_v.at[j]])
            pltpu.sync_copy(buf, out_hbm.at[d1_v.at[j]])

    shp = (SC_WORKERS, n_sub, SC_CHUNK)
    return k(hp, dest0.reshape(shp), dest1.reshape(shp))


def _sc_gather(table, idx):
    M = idx.shape[0]
    W = table.shape[1]
    per_w, n_sub = _sc_rows(M)
    assert n_sub % 2 == 0
    mesh = plsc.VectorSubcoreMesh(core_axis_name="c", subcore_axis_name="s")

    @functools.partial(
        pl.kernel, mesh=mesh,
        out_type=jax.ShapeDtypeStruct((M, W), table.dtype),
        scratch_types=[pltpu.VMEM((n_sub, SC_CHUNK), jnp.int32),
                       pltpu.VMEM((SC_CHUNK, W), table.dtype), pltpu.VMEM((SC_CHUNK, W), table.dtype),
                       pltpu.SemaphoreType.DMA, pltpu.SemaphoreType.DMA],
    )
    def k(table_hbm, idx_hbm, out_hbm, idx_v, buf0, buf1, sem0, sem1):
        wid = lax.axis_index("s") * SC_CORES + lax.axis_index("c")
        pltpu.sync_copy(idx_hbm.at[wid], idx_v)
        bufs, sems = (buf0, buf1), (sem0, sem1)

        def gather(j, slot):
            return pltpu.make_async_copy(table_hbm.at[idx_v.at[j]], bufs[slot], sems[slot])

        gather(0, 0).start()

        @pl.loop(0, n_sub // 2)
        def _(jj):
            for slot in range(2):
                j = jj * 2 + slot

                @pl.when(j + 1 < n_sub)
                def _():
                    gather(j + 1, 1 - slot).start()

                gather(j, slot).wait()
                pltpu.sync_copy(bufs[slot], out_hbm.at[pl.ds(wid * per_w + j * SC_CHUNK, SC_CHUNK)])

    return k(table, idx.reshape(SC_WORKERS, n_sub, SC_CHUNK))


def _moe(x1, hp, ri, rg, cnt, layer, wg, wu, wd, gfinal, final):
    T = x1.shape[0]
    bm = MOE_ROWS
    experts = jnp.arange(N_EXPERTS, dtype=jnp.int32)
    counts = cnt[0, :N_EXPERTS]
    padded = (counts + bm - 1) // bm * bm
    pad_ends = jnp.cumsum(padded)
    pad_starts = pad_ends - padded
    dest2 = ri[:, 2:4] + jnp.sum(jnp.where(ri[:, 0:2, None] > experts, padded, 0), axis=-1)
    n_blocks = (2 * T) // bm + N_EXPERTS
    block_start = jnp.arange(n_blocks, dtype=jnp.int32) * bm
    block_expert = jnp.minimum(jnp.sum((pad_ends[None, :] <= block_start[:, None]).astype(jnp.int32), axis=1),
                               N_EXPERTS - 1)
    block_rows = jnp.clip(counts[block_expert] - (block_start - pad_starts[block_expert]), 0, bm)
    xs = _sc_dispatch(hp, dest2[:, 0], dest2[:, 1], n_blocks * bm)
    y_rows = _expert_blocks(xs, block_expert, block_rows, layer, wg, wu, wd)
    yg = _sc_gather(y_rows, jnp.concatenate([dest2[:, 0], dest2[:, 1]]))
    return _moe_combine(x1, yg, rg, gfinal, final)


def _ssd_pre_kernel(x_ref, g_ref, w_ref, z_ref, xbc_ref, dt_ref):
    xn = _rms(x_ref[...], g_ref[...]).astype(BF16)
    step = 512
    for c in range(0, D_INNER, step):
        z_ref[:, c:c + step] = _dot(xn, w_ref[:, c:c + step]).astype(BF16)
    for c in range(0, XBC_DIM, step):
        xbc_ref[:, c:c + step] = _dot(xn, w_ref[:, D_INNER + c:D_INNER + c + step]).astype(BF16)
    dt_ref[...] = _dot(xn, w_ref[:, D_INNER + XBC_DIM:])


def _ssd_pre(x2, g, w_cat):
    T = x2.shape[0]
    tm = min(ROW_TILE, T)
    row = lambda n: pl.BlockSpec((tm, n), lambda i: (i, 0))
    return pl.pallas_call(
        _ssd_pre_kernel,
        grid=(T // tm,),
        in_specs=[row(D_MODEL), _full((1, D_MODEL)), _full(w_cat.shape)],
        out_specs=[row(D_INNER), row(XBC_DIM), row(LANES)],
        out_shape=[jax.ShapeDtypeStruct((T, D_INNER), BF16), jax.ShapeDtypeStruct((T, XBC_DIM), BF16),
                   jax.ShapeDtypeStruct((T, LANES), F32)],
        compiler_params=_params("parallel"),
        name="ssd_pre",
    )(x2, g, w_cat)


def _pair_expand(v, j, low):
    return jnp.where(low, v[:, 2 * j:2 * j + 1], v[:, 2 * j + 1:2 * j + 2])


def _ssd_scan_kernel(xbc_ref, dt_ref, cw_ref, cb_ref, dtb_ref, alog_ref, dsk_ref, y_ref,
                     ubuf, abuf, state):
    Q = CHUNK
    c = pl.program_id(1)

    @pl.when(c == 0)
    def _():
        ubuf[0:8, :] = jnp.zeros((8, XBC_DIM), F32)
        state[...] = jnp.zeros(state.shape, F32)

    ubuf[8:8 + Q, :] = xbc_ref[0].astype(F32)
    step = 512
    for cc in range(0, XBC_DIM, step):
        sl = slice(cc, cc + step)
        acc = cb_ref[:, sl] + cw_ref[3:4, sl] * ubuf[8:8 + Q, sl]
        for kk in range(D_CONV - 1):
            acc = acc + cw_ref[kk:kk + 1, sl] * ubuf[5 + kk:5 + kk + Q, sl]
        abuf[:, sl] = acc * _sigmoid(acc)
    ubuf[0:8, :] = ubuf[Q:Q + 8, :]

    x_dt = dt_ref[0] + dtb_ref[...]
    dt = jnp.maximum(x_dt, 0.0) + jnp.log(1.0 + jnp.exp(-jnp.abs(x_dt)))
    da = dt * (-jnp.exp(alog_ref[...]))
    ri = lax.broadcasted_iota(jnp.int32, (Q, Q), 0)
    ci = lax.broadcasted_iota(jnp.int32, (Q, Q), 1)
    causal = ci <= ri
    tril = jnp.where(causal, 1.0, 0.0).astype(BF16)
    d_hi = da.astype(BF16)
    r1 = da - d_hi.astype(F32)
    d_mid = r1.astype(BF16)
    d_lo = (r1 - d_mid.astype(F32)).astype(BF16)
    cs = _dot(tril, d_hi) + _dot(tril, d_mid) + _dot(tril, d_lo)
    cs_t = jnp.transpose(cs)
    cs_last = cs[Q - 1:Q, :]
    ecs = jnp.exp(cs)
    to_end = jnp.exp(cs_last - cs)
    dec_last = jnp.exp(cs_last)
    low = lax.broadcasted_iota(jnp.int32, (Q, LANES), 1) < SSD_HEADDIM
    low1 = low[0:1, :]
    hg = SSD_HEADS // SSD_GROUPS
    gw = hg * SSD_HEADDIM
    for g in range(SSD_GROUPS):
        b_g = abuf[:, D_INNER + g * D_STATE:D_INNER + (g + 1) * D_STATE]
        c_g = abuf[:, D_INNER + SSD_GROUPS * D_STATE + g * D_STATE:
                   D_INNER + SSD_GROUPS * D_STATE + (g + 1) * D_STATE].astype(BF16)
        cb = _dot_nt(c_g, b_g.astype(BF16))
        b_t = jnp.transpose(b_g).astype(BF16)
        st = state[g]
        y_inter = _dot(c_g, st.astype(BF16))
        xw_parts, dl_parts = [], []
        for p in range(hg // 2):
            j = g * (hg // 2) + p
            lanes = slice(j * LANES, (j + 1) * LANES)
            x_p = abuf[:, lanes]
            xdt = x_p * _pair_expand(dt, j, low)
            ys = []
            for e in range(2):
                h = 2 * j + e
                seg = cs[:, h:h + 1] - cs_t[h:h + 1, :]
                lmat = jnp.exp(jnp.where(causal, seg, NEG_INF))
                ys.append(_dot((cb * lmat).astype(BF16), xdt.astype(BF16)))
            y_p = (jnp.where(low, ys[0], ys[1])
                   + y_inter[:, p * LANES:(p + 1) * LANES] * _pair_expand(ecs, j, low)
                   + dsk_ref[:, lanes] * x_p)
            y_ref[0, :, lanes] = y_p.astype(y_ref.dtype)
            xw_parts.append((xdt * _pair_expand(to_end, j, low)).astype(BF16))
            dl_parts.append(_pair_expand(dec_last, j, low1))
        xw = jnp.concatenate(xw_parts, axis=1)
        dl = jnp.concatenate(dl_parts, axis=1)
        state[g] = st * dl + _dot(b_t, xw)
    del gw


def _ssd_scan(xbc, dt, conv_w, conv_b, dt_bias, a_log, d_skip):
    B, S, _ = xbc.shape
    Q = CHUNK
    hg = SSD_HEADS // SSD_GROUPS
    return pl.pallas_call(
        _ssd_scan_kernel,
        grid=(B, S // Q),
        in_specs=[pl.BlockSpec((1, Q, XBC_DIM), lambda b, c: (b, c, 0)),
                  pl.BlockSpec((1, Q, LANES), lambda b, c: (b, c, 0)),
                  _full((D_CONV, XBC_DIM)), _full((1, XBC_DIM)), _full((1, LANES)), _full((1, LANES)),
                  _full((1, D_INNER))],
        out_specs=pl.BlockSpec((1, Q, D_INNER), lambda b, c: (b, c, 0)),
        out_shape=jax.ShapeDtypeStruct((B, S, D_INNER), BF16),
        scratch_shapes=[pltpu.VMEM((Q + 8, XBC_DIM), F32), pltpu.VMEM((Q, XBC_DIM), F32),
                        pltpu.VMEM((SSD_GROUPS, D_STATE, hg * SSD_HEADDIM), F32)],
        compiler_params=_params("parallel", "arbitrary"),
        name="ssd_scan",
    )(xbc, dt, conv_w, conv_b, dt_bias, a_log, d_skip)


def _ssd_post_kernel(y_ref, z_ref, gn_ref, x_ref, wo_ref, *tail_refs):
    gsz = D_INNER // SSD_GROUPS
    out = None
    for g in range(SSD_GROUPS):
        sl = slice(g * gsz, (g + 1) * gsz)
        z = z_ref[:, sl].astype(F32)
        yz = y_ref[:, sl].astype(F32) * (z * _sigmoid(z))
        part = _dot(_rms(yz, gn_ref[:, sl]).astype(BF16), wo_ref[sl, :])
        out = part if out is None else out + part
    _route_tail(x_ref[...] + out, *tail_refs)


def _ssd_post(y, z, gn, x2, wo, tail):
    T = x2.shape[0]
    tm = min(ROW_TILE, T)
    row = lambda n: pl.BlockSpec((tm, n), lambda i: (i, 0))
    t_in, t_out = _tail_specs(tm)
    return pl.pallas_call(
        _ssd_post_kernel,
        grid=(T // tm,),
        in_specs=[row(D_INNER), row(D_INNER), _full((1, D_INNER)), row(D_MODEL), _full(wo.shape)] + t_in,
        out_specs=t_out,
        out_shape=_tail_shapes(T),
        scratch_shapes=[pltpu.VMEM((1, LANES), F32)],
        compiler_params=_params("arbitrary"),
        name="ssd_post",
    )(y, z, gn, x2, wo, *tail)


def _rope_tables(positions):
    pos = positions.reshape(-1, 1).astype(F32)
    lane = jnp.arange(LANES)

    def tables(rot, lead, period):
        half = rot // 2
        off = lane % period - lead
        rotary = (off >= 0) & (off < rot)
        inv = ROPE_THETA ** (-(2 * (off % half)).astype(F32) / rot)
        ang = pos * jnp.where(rotary, inv, 0.0)[None, :]
        sign = jnp.where(rotary, jnp.where(off < half, -1.0, 1.0), 0.0)
        return [jnp.cos(ang), jnp.sin(ang) * sign[None, :]]

    return tables(ROPE_DIM_A, 0, HEAD_DIM_A) + tables(MLA_ROPE, MLA_NOPE, LANES)


def _pad_cols(w, n):
    return jnp.pad(w, ((0, 0), (0, n - w.shape[1])))


def _head_pad(w, width):
    K = w.shape[0]
    w = w.reshape(K, -1, width)
    return jnp.pad(w, ((0, 0), (0, 0), (0, LANES - width))).reshape(K, -1)


def _router_weights(w_group, b_group, w_router, b_router):
    w = _pad_cols(jnp.concatenate([w_router, w_group], axis=1), LANES)
    b = _pad_cols(jnp.concatenate([b_router, b_group])[None, :], LANES)
    hi = w.astype(BF16)
    lo = (w - hi.astype(F32)).astype(BF16)
    return hi, lo, b


def kernel(x, positions, attn_norm, w_in_attn, mla_q_norm, w_uq, mla_kv_norm, w_ukv, w_out_attn,
           ssd_norm, w_in_ssd, conv_w, conv_b, dt_bias, a_log, d_skip, gate_norm, w_out_ssd,
           moe_norm, w_group, b_group, w_router, b_router, w_gate, w_up, w_down, final_norm):
    B, S, D = x.shape
    T = B * S
    x2 = x.reshape(T, D)
    tabs = _rope_tables(positions)

    def tail_params(layer):
        hi, lo, b = _router_weights(w_group[layer], b_group[layer], w_router[layer], b_router[layer])
        return [moe_norm[layer][None, :], hi, lo, b]

    w_in = w_in_attn[0]
    kr_off = 3 * MIX_A + MLA_Q_RANK + MLA_KV_RANK
    kr_cols = jnp.pad(w_in[:, kr_off:], ((0, 0), (MLA_NOPE, LANES - MLA_NOPE - MLA_ROPE)))
    w_cat = jnp.concatenate([w_in[:, :kr_off], kr_cols], axis=1).astype(BF16)
    wuq = _head_pad(w_uq[0], MLA_NOPE + MLA_ROPE).astype(BF16)
    ukv = w_ukv[0].reshape(MLA_KV_RANK, MLA_HEADS, MLA_NOPE + MLA_V)
    wuk = _head_pad(ukv[:, :, :MLA_NOPE].reshape(MLA_KV_RANK, -1), MLA_NOPE).astype(BF16)
    wuv = _head_pad(ukv[:, :, MLA_NOPE:].reshape(MLA_KV_RANK, -1), MLA_V).astype(BF16)
    qa, ka, va, qm, km, vm = _attn_pre(x2, attn_norm[0][None, :], w_cat, mla_q_norm[0][None, :], wuq,
                                       mla_kv_norm[0][None, :], wuk, wuv, tabs)
    shp = lambda t: t.reshape(B, S, t.shape[-1])
    prev = None
    for d in DILATIONS:
        prev = _dilated_branch(shp(qa), shp(ka), shp(va), d, prev)
    oa = prev[0].reshape(T, MIX_A)
    ob = _mla_attention(shp(qm), shp(km), shp(vm)).reshape(T, MIX_B)
    x1, h, ri, rg, cnt = _attn_post(oa, ob, x2, w_out_attn[0].astype(BF16), tail_params(0))
    x2 = _moe(x1, h, ri, rg, cnt, 0, w_gate, w_up, w_down, final_norm[None, :], False)

    w_cat = _pad_cols(w_in_ssd[0], D_INNER + XBC_DIM + LANES).astype(BF16)
    z, xbc, dt = _ssd_pre(x2, ssd_norm[0][None, :], w_cat)
    y = _ssd_scan(xbc.reshape(B, S, XBC_DIM), dt.reshape(B, S, LANES), conv_w[0], conv_b[0][None, :],
                  _pad_cols(dt_bias[0][None, :], LANES), _pad_cols(a_log[0][None, :], LANES),
                  jnp.repeat(d_skip[0], SSD_HEADDIM)[None, :])
    x1, h, ri, rg, cnt = _ssd_post(y.reshape(T, D_INNER), z, gate_norm[0][None, :], x2,
                                   w_out_ssd[0].astype(BF16), tail_params(1))
    out = _moe(x1, h, ri, rg, cnt, 1, w_gate, w_up, w_down, final_norm[None, :], True)
    return out.reshape(B, S, D)
```

```python
import functools

import jax
import jax.numpy as jnp
from jax import lax
from jax.experimental import pallas as pl
from jax.experimental.pallas import tpu as pltpu
from jax.experimental.pallas import tpu_sc as plsc

F32 = jnp.float32
BF16 = jnp.bfloat16

D_MODEL = 1024
HEADS_A = 8
HEAD_DIM_A = 64
ROPE_DIM_A = 16
DILATIONS = (1, 4, 16)
SPAN = 128
DIL_ROWS = 2048
MLA_HEADS = 8
MLA_NOPE = 64
MLA_ROPE = 32
MLA_V = 64
MLA_Q_RANK = 256
MLA_KV_RANK = 128
ROPE_THETA = 500000.0
MIX_A = HEADS_A * HEAD_DIM_A
MIX_B = MLA_HEADS * MLA_V
D_INNER = 2048
SSD_HEADDIM = 64
SSD_HEADS = 32
SSD_GROUPS = 4
D_STATE = 128
D_CONV = 4
CHUNK = 256
XBC_DIM = D_INNER + 2 * SSD_GROUPS * D_STATE
N_GROUPS = 4
EXPERTS_PER_GROUP = 8
N_EXPERTS = 32
D_EXPERT = 512
NORM_EPS = 1e-6

LANES = 128
VMEM_LIMIT = 56 * 1024 * 1024
ROW_TILE = 512
MOE_ROWS = 256
SC_CORES = 2
SC_WORKERS = 32
SC_CHUNK = 64
MLA_TQ = 512
MLA_TK = 512
NEG_INF = float("-inf")
LOG2E = 1.4426950408889634


def _params(*sem):
    return pltpu.CompilerParams(dimension_semantics=sem, vmem_limit_bytes=VMEM_LIMIT)


def _dot(a, b):
    return jnp.dot(a, b, preferred_element_type=F32)


def _dot_nt(a, b):
    return lax.dot_general(a, b, (((1,), (1,)), ((), ())), preferred_element_type=F32)


def _rms(x, g):
    return x * lax.rsqrt(jnp.mean(x * x, axis=-1, keepdims=True) + NORM_EPS) * g


def _sigmoid(x):
    return 1.0 / (1.0 + jnp.exp(-x))


def _rope(t, c, s, half, first):
    partner = jnp.where(first, pltpu.roll(t, LANES - half, 1), pltpu.roll(t, half, 1))
    return t * c + partner * s


def _pack_bf16_pairs(x):
    n = x.shape[1] // 2
    bits = lax.bitcast_convert_type(x.astype(BF16).astype(F32), jnp.uint32)
    return (bits[:, :n] >> 16) | (bits[:, n:] & jnp.uint32(0xFFFF0000))


def _unpack_bf16_pairs(w):
    lo = lax.bitcast_convert_type(w << 16, F32)
    hi = lax.bitcast_convert_type(w & jnp.uint32(0xFFFF0000), F32)
    return jnp.concatenate([lo, hi], axis=1)


def _full(shape):
    return pl.BlockSpec(shape, lambda *_: (0,) * len(shape))


def _attn_pre_kernel(x_ref, g_ref, w_ref, qn_ref, wuq_ref, kvn_ref, wuk_ref, wuv_ref,
                     ca_ref, sa_ref, cb_ref, sb_ref,
                     qa_ref, ka_ref, va_ref, q4_ref, k4_ref, v4_ref, q16_ref, k16_ref, v16_ref,
                     qm_ref, km_ref, vm_ref, lay):
    xn = _rms(x_ref[...], g_ref[...]).astype(BF16)
    ca, sa, cb, sb = ca_ref[...], sa_ref[...], cb_ref[...], sb_ref[...]
    half_a = ROPE_DIM_A // 2
    half_b = MLA_ROPE // 2
    lane = lax.broadcasted_iota(jnp.int32, ca.shape, 1)
    first_a = lane % HEAD_DIM_A < half_a
    first_b = (lane >= MLA_NOPE) & (lane < MLA_NOPE + half_b)

    q = _dot(xn, w_ref[:, 0:MIX_A]) * (HEAD_DIM_A ** -0.5)
    k = _dot(xn, w_ref[:, MIX_A:2 * MIX_A])
    v = _dot(xn, w_ref[:, 2 * MIX_A:3 * MIX_A])
    for c in range(MIX_A // LANES):
        sl = slice(c * LANES, (c + 1) * LANES)
        lay[0, c] = _rope(q[:, sl], ca, sa, half_a, first_a)
        lay[1, c] = _rope(k[:, sl], ca, sa, half_a, first_a)
        lay[2, c] = v[:, sl]
    tm = x_ref.shape[0]
    for a, (nat, by4, by16) in enumerate(((qa_ref, q4_ref, q16_ref), (ka_ref, k4_ref, k16_ref),
                                          (va_ref, v4_ref, v16_ref))):
        for c in range(MIX_A // LANES):
            sl = slice(c * LANES, (c + 1) * LANES)
            nat[:, sl] = lay[a, c].astype(BF16)
            for d, ref in ((4, by4), (16, by16)):
                for r in range(d):
                    ref[0, r, :, sl] = lay[a, c, pl.ds(r, tm // d, stride=d), :].astype(BF16)

    o = 3 * MIX_A
    cq = _dot(xn, w_ref[:, o:o + MLA_Q_RANK])
    ckv = _dot(xn, w_ref[:, o + MLA_Q_RANK:o + MLA_Q_RANK + MLA_KV_RANK])
    kr = _dot(xn, w_ref[:, o + MLA_Q_RANK + MLA_KV_RANK:o + MLA_Q_RANK + MLA_KV_RANK + LANES])
    cqn = _rms(cq, qn_ref[...]).astype(BF16)
    kvn = _rms(ckv, kvn_ref[...]).astype(BF16)
    krr = _rope(kr, cb, sb, half_b, first_b)
    qm = _dot(cqn, wuq_ref[...]) * ((MLA_NOPE + MLA_ROPE) ** -0.5 * LOG2E)
    km = _dot(kvn, wuk_ref[...])
    for h in range(MLA_HEADS):
        sl = slice(h * LANES, (h + 1) * LANES)
        qm_ref[:, sl] = _rope(qm[:, sl], cb, sb, half_b, first_b).astype(BF16)
        km_ref[:, sl] = (km[:, sl] + krr).astype(BF16)
    vm = _dot(kvn, wuv_ref[...])
    low = lax.broadcasted_iota(jnp.int32, vm.shape, 1) % LANES < MLA_V
    vm_ref[...] = jnp.where(low, vm, 1.0).astype(BF16)


def _attn_pre(x2, g, w_cat, qn, wuq, kvn, wuk, wuv, tabs, seq):
    T = x2.shape[0]
    tm = min(ROW_TILE, T)
    nb = seq // tm
    row = lambda n: pl.BlockSpec((tm, n), lambda i: (i, 0))
    wide = MLA_HEADS * LANES
    by_residue = [pl.BlockSpec((1, d, tm // d, MIX_A), lambda i: (i // nb, 0, i % nb, 0))
                  for d in DILATIONS[1:]]
    residue_shapes = [jax.ShapeDtypeStruct((T // seq, d, seq // d, MIX_A), BF16) for d in DILATIONS[1:]]
    return pl.pallas_call(
        _attn_pre_kernel,
        grid=(T // tm,),
        in_specs=[row(D_MODEL), _full((1, D_MODEL)), _full(w_cat.shape), _full((1, MLA_Q_RANK)),
                  _full(wuq.shape), _full((1, MLA_KV_RANK)), _full(wuk.shape), _full(wuv.shape)]
                 + [row(LANES)] * 4,
        out_specs=[row(MIX_A)] * 3 + [by_residue[0]] * 3 + [by_residue[1]] * 3 + [row(wide)] * 3,
        out_shape=[jax.ShapeDtypeStruct((T, MIX_A), BF16)] * 3 + [residue_shapes[0]] * 3
                  + [residue_shapes[1]] * 3 + [jax.ShapeDtypeStruct((T, wide), BF16)] * 3,
        scratch_shapes=[pltpu.VMEM((3, MIX_A // LANES, tm, LANES), F32)],
        compiler_params=_params("parallel"),
        name="attn_pre",
    )(x2, g, w_cat, qn, wuq, kvn, wuk, wuv, *tabs)


def _dilated_kernel(*refs, d, tq, carry, last):
    refs = list(refs)
    q_ref, kc_ref, kp_ref, vc_ref, vp_ref = refs[:5]
    op_ref, lp_ref = refs[5:7] if carry else (None, None)
    kbuf, vbuf = refs[-2:]
    o_ref = refs[7] if carry else refs[5]
    lse_ref = None if last else refs[-3]
    j = pl.program_id(2)
    for r in range(d):
        kbuf[r, 0:SPAN, :] = kp_ref[0, r]
        kbuf[r, SPAN:, :] = kc_ref[0, r]
        vbuf[r, 0:SPAN, :] = vp_ref[0, r]
        vbuf[r, SPAN:, :] = vc_ref[0, r]
    low = lax.broadcasted_iota(jnp.int32, (SPAN, LANES), 1) < HEAD_DIM_A
    qi = lax.broadcasted_iota(jnp.int32, (SPAN, 2 * SPAN), 0)
    ki = lax.broadcasted_iota(jnp.int32, (SPAN, 2 * SPAN), 1)
    band = (ki >= qi) & (ki <= qi + SPAN)
    first_valid = band & ((ki >= SPAN) | (j > 0))
    for r in range(d):
        for sub in range(tq // SPAN):
            valid = first_valid if sub == 0 else band
            q2 = q_ref[0, r, sub * SPAN:(sub + 1) * SPAN, :]
            kk = kbuf[r, sub * SPAN:(sub + 2) * SPAN, :]
            vv = vbuf[r, sub * SPAN:(sub + 2) * SPAN, :]
            os_, ls_ = [], []
            for e in range(2):
                qe = jnp.where(low if e == 0 else jnp.logical_not(low), q2, jnp.zeros_like(q2))
                s = jnp.where(valid, _dot_nt(qe, kk), NEG_INF)
                m = jnp.max(s, axis=-1, keepdims=True)
                pr = jnp.exp(s - m)
                l = jnp.sum(pr, axis=-1, keepdims=True)
                os_.append(_dot(pr.astype(BF16), vv) / l)
                ls_.append(m + jnp.log(l))
            o2 = jnp.where(low, os_[0], os_[1])
            l2 = jnp.where(low, ls_[0], ls_[1])
            rows = pl.ds(sub * SPAN, SPAN) if d == 1 else pl.ds(d * sub * SPAN + r, SPAN, stride=d)
            if carry:
                lp = lp_ref[0, 0, rows, :]
                mx = jnp.maximum(lp, l2)
                wp = jnp.exp(lp - mx)
                wc = jnp.exp(l2 - mx)
                den = wp + wc
                o2 = (wp * op_ref[0, 0, rows, :] + wc * o2) / den
                l2 = mx + jnp.log(den)
            o_ref[0, 0, rows, :] = o2
            if not last:
                lse_ref[0, 0, rows, :] = l2


def _dilated_branch(q, k, v, d, prev, last):
    B, _, L, _ = q.shape
    pairs = MIX_A // LANES
    tq = DIL_ROWS // d
    cur = pl.BlockSpec((1, d, tq, LANES), lambda b, p, j: (b, 0, j, p))
    prv = pl.BlockSpec((1, d, SPAN, LANES), lambda b, p, j: (b, 0, jnp.maximum(j * (tq // SPAN) - 1, 0), p))
    nat = pl.BlockSpec((1, 1, DIL_ROWS, LANES), lambda b, p, j: (b, p, j, 0))
    nat_shape = jax.ShapeDtypeStruct((B, pairs, d * L, LANES), F32)
    carry = prev is not None
    outs = pl.pallas_call(
        functools.partial(_dilated_kernel, d=d, tq=tq, carry=carry, last=last),
        grid=(B, pairs, L // tq),
        in_specs=[cur, cur, prv, cur, prv] + ([nat, nat] if carry else []),
        out_specs=[nat] if last else [nat, nat],
        out_shape=[nat_shape] if last else [nat_shape, nat_shape],
        scratch_shapes=[pltpu.VMEM((d, tq + SPAN, LANES), BF16), pltpu.VMEM((d, tq + SPAN, LANES), BF16)],
        compiler_params=_params("parallel", "parallel", "parallel"),
        name=f"dilated_d{d}",
    )(q, k, k, v, v, *(prev if carry else ()))
    return outs


def _mla_kernel(q_ref, k_ref, v_ref, o_ref, m0, m1, a0, a1, *, tq, tk):
    qi = pl.program_id(2)
    ms, accs = (m0, m1), (a0, a1)
    for e in range(2):
        ms[e][...] = jnp.full(ms[e].shape, NEG_INF, F32)
        accs[e][...] = jnp.zeros(accs[e].shape, F32)
    row = lax.broadcasted_iota(jnp.int32, (tq, tk), 0)
    col = lax.broadcasted_iota(jnp.int32, (tq, tk), 1)

    def step(kv, masked):
        start = pl.multiple_of(kv * tk, tk)
        for e in range(2):
            lanes = slice(e * LANES, (e + 1) * LANES)
            s = _dot_nt(q_ref[0, :, lanes], k_ref[0, pl.ds(start, tk), lanes])
            if masked:
                s = jnp.where(col <= row, s, NEG_INF)
            m_prev = ms[e][...]
            m_new = jnp.maximum(m_prev, jnp.max(s, axis=-1, keepdims=True))
            alpha = jnp.exp2(m_prev - m_new)
            pr = jnp.concatenate([jnp.exp2(s[:, c * LANES:(c + 1) * LANES] - m_new)
                                  for c in range(tk // LANES)], axis=1).astype(BF16)
            accs[e][...] = alpha * accs[e][...] + _dot(pr, v_ref[0, pl.ds(start, tk), lanes])
            ms[e][...] = m_new

    n_full = qi * (tq // tk)

    def body(i, c):
        step(2 * i, False)
        step(2 * i + 1, False)
        return c

    lax.fori_loop(0, n_full // 2, body, 0)

    @pl.when(n_full % 2 == 1)
    def _():
        step(n_full - 1, False)

    for t in range(tq // tk):
        step(qi * (tq // tk) + t, True)
    low = lax.broadcasted_iota(jnp.int32, (tq, LANES), 1) < MLA_V
    r0 = a0[...] / pltpu.roll(a0[...], MLA_V, 1)
    r1 = a1[...] / pltpu.roll(a1[...], MLA_V, 1)
    o_ref[0] = jnp.where(low, r0, pltpu.roll(r1, MLA_V, 1)).astype(BF16)


def _mla_attention(qm, km, vm):
    B, S, _ = qm.shape
    tq = min(MLA_TQ, S)
    tk = tq
    pairs = MLA_HEADS // 2
    return pl.pallas_call(
        functools.partial(_mla_kernel, tq=tq, tk=tk),
        grid=(B, pairs, S // tq),
        in_specs=[pl.BlockSpec((1, tq, 2 * LANES), lambda b, p, i: (b, i, p)),
                  pl.BlockSpec((1, S, 2 * LANES), lambda b, p, i: (b, 0, p)),
                  pl.BlockSpec((1, S, 2 * LANES), lambda b, p, i: (b, 0, p))],
        out_specs=pl.BlockSpec((1, tq, LANES), lambda b, p, i: (b, i, p)),
        out_shape=jax.ShapeDtypeStruct((B, S, MIX_B), BF16),
        scratch_shapes=[pltpu.VMEM((tq, LANES), F32)] * 4,
        compiler_params=_params("parallel", "parallel", "arbitrary"),
        name="mla_flash",
    )(qm, km, vm)


def _route_tail(x1, gm_ref, wrh_ref, wrl_ref, br_ref, x1_ref, h_ref, ri_ref, rg_ref, cnt_ref, cnt_sc):
    x1_ref[...] = x1
    hn = _rms(x1, gm_ref[...])
    h_hi = hn.astype(BF16)
    h_ref[...] = _pack_bf16_pairs(hn)
    h_lo = (hn - h_hi.astype(F32)).astype(BF16)
    logits = (_dot(h_hi, wrh_ref[...]) + _dot(h_hi, wrl_ref[...]) + _dot(h_lo, wrh_ref[...])
              + br_ref[...])
    tm = logits.shape[0]
    lane = lax.broadcasted_iota(jnp.int32, (tm, LANES), 1)
    lanef = lane.astype(F32)
    big = float(LANES)
    is_g = (lane >= N_EXPERTS) & (lane < N_EXPERTS + N_GROUPS)
    gl = jnp.where(is_g, logits, NEG_INF)
    gmax = jnp.max(gl, axis=-1, keepdims=True)
    gsum = jnp.sum(jnp.exp(gl - gmax), axis=-1, keepdims=True)
    g_val = 1.0 / gsum
    g_idx = jnp.min(jnp.where(gl == gmax, lanef, big), axis=-1, keepdims=True) - float(N_EXPERTS)
    in_grp = (lane < N_EXPERTS) & ((lane // EXPERTS_PER_GROUP).astype(F32) == g_idx)
    el = jnp.where(in_grp, logits, NEG_INF)
    e1 = jnp.max(el, axis=-1, keepdims=True)
    i1 = jnp.min(jnp.where(el == e1, lanef, big), axis=-1, keepdims=True)
    el2 = jnp.where(lanef == i1, NEG_INF, el)
    e2 = jnp.max(el2, axis=-1, keepdims=True)
    i2 = jnp.min(jnp.where(el2 == e2, lanef, big), axis=-1, keepdims=True)
    t = jnp.exp(e2 - e1)
    p1 = 1.0 / (1.0 + t)
    rg_ref[...] = jnp.where(lane == 0, p1 * g_val, jnp.where(lane == 1, t * p1 * g_val, 0.0))
    @pl.when(pl.program_id(0) == 0)
    def _():
        cnt_sc[...] = jnp.zeros(cnt_sc.shape, F32)

    oh1 = lanef == i1
    oh2 = lanef == i2
    oh = jnp.where(oh1 | oh2, 1.0, 0.0)
    rr = lax.broadcasted_iota(jnp.int32, (tm, tm), 0)
    cc = lax.broadcasted_iota(jnp.int32, (tm, tm), 1)
    strict = jnp.where(cc < rr, 1.0, 0.0).astype(BF16)
    before = _dot(strict, oh.astype(BF16)) + cnt_sc[...]
    rank1 = jnp.sum(jnp.where(oh1, before, 0.0), axis=-1, keepdims=True)
    rank2 = jnp.sum(jnp.where(oh2, before, 0.0), axis=-1, keepdims=True)
    cnt_sc[...] = cnt_sc[...] + jnp.sum(oh, axis=0, keepdims=True)
    cnt_ref[...] = jnp.broadcast_to(cnt_sc[...], cnt_ref.shape).astype(jnp.int32)
    ri_ref[...] = jnp.where(lane == 0, i1, jnp.where(lane == 1, i2, jnp.where(
        lane == 2, rank1, jnp.where(lane == 3, rank2, 0.0)))).astype(jnp.int32)


def _tail_specs(tm):
    row = lambda n: pl.BlockSpec((tm, n), lambda i: (i, 0))
    in_specs = [_full((1, D_MODEL)), _full((D_MODEL, LANES)), _full((D_MODEL, LANES)), _full((1, LANES))]
    out_specs = [row(D_MODEL), row(D_MODEL // 2), row(LANES), row(LANES), _full((8, LANES))]
    return in_specs, out_specs


def _tail_shapes(T):
    return [jax.ShapeDtypeStruct((T, D_MODEL), F32), jax.ShapeDtypeStruct((T, D_MODEL // 2), jnp.uint32),
            jax.ShapeDtypeStruct((T, LANES), jnp.int32), jax.ShapeDtypeStruct((T, LANES), F32),
            jax.ShapeDtypeStruct((8, LANES), jnp.int32)]


def _attn_post_kernel(oa_ref, ob_ref, x_ref, wo_ref, *tail_refs):
    oa = jnp.concatenate([oa_ref[0, p] for p in range(MIX_A // LANES)], axis=1).astype(BF16)
    y = _dot(oa, wo_ref[0:MIX_A, :]) + _dot(ob_ref[...], wo_ref[MIX_A:, :])
    _route_tail(x_ref[...] + y, *tail_refs)


def _attn_post(oa, ob, x2, wo, tail):
    T = x2.shape[0]
    tm = min(ROW_TILE, T)
    nb = oa.shape[2] // tm
    row = lambda n: pl.BlockSpec((tm, n), lambda i: (i, 0))
    t_in, t_out = _tail_specs(tm)
    return pl.pallas_call(
        _attn_post_kernel,
        grid=(T // tm,),
        in_specs=[pl.BlockSpec((1, MIX_A // LANES, tm, LANES), lambda i: (i // nb, 0, i % nb, 0)),
                  row(MIX_B), row(D_MODEL), _full(wo.shape)] + t_in,
        out_specs=t_out,
        out_shape=_tail_shapes(T),
        scratch_shapes=[pltpu.VMEM((1, LANES), F32)],
        compiler_params=_params("arbitrary"),
        name="attn_post",
    )(oa, ob, x2, wo, *tail)


def _expert_kernel(be_ref, nr_ref, xs_ref, wg_ref, wu_ref, wd_ref, y_ref, wg_bf, wu_bf, wd_bf):
    i = pl.program_id(0)

    @pl.when((i == 0) | (be_ref[i] != be_ref[jnp.maximum(i - 1, 0)]))
    def _():
        wg_bf[...] = wg_ref[0].astype(BF16)
        wu_bf[...] = wu_ref[0].astype(BF16)
        wd_bf[...] = wd_ref[0].astype(BF16)

    @pl.when(nr_ref[i] > 0)
    def _():
        live = lax.broadcasted_iota(jnp.int32, xs_ref.shape, 0) < nr_ref[i]
        xb = _unpack_bf16_pairs(jnp.where(live, xs_ref[...], jnp.uint32(0))).astype(BF16)
        g = _dot(xb, wg_bf[...])
        u = _dot(xb, wu_bf[...])
        hid = (g * _sigmoid(g) * u).astype(BF16)
        y_ref[...] = _pack_bf16_pairs(_dot(hid, wd_bf[...]))

    @pl.when(nr_ref[i] == 0)
    def _():
        y_ref[...] = jnp.zeros(y_ref.shape, y_ref.dtype)


def _expert_blocks(xs, block_expert, block_rows, layer, wg, wu, wd):
    rows = xs.shape[0]
    bm = MOE_ROWS
    half = D_MODEL // 2
    grid_spec = pltpu.PrefetchScalarGridSpec(
        num_scalar_prefetch=2,
        grid=(rows // bm,),
        in_specs=[pl.BlockSpec((bm, half), lambda i, be, nr: (i, 0)),
                  pl.BlockSpec((None, 1, D_MODEL, D_EXPERT), lambda i, be, nr: (layer, be[i], 0, 0)),
                  pl.BlockSpec((None, 1, D_MODEL, D_EXPERT), lambda i, be, nr: (layer, be[i], 0, 0)),
                  pl.BlockSpec((None, 1, D_EXPERT, D_MODEL), lambda i, be, nr: (layer, be[i], 0, 0))],
        out_specs=pl.BlockSpec((bm, half), lambda i, be, nr: (i, 0)),
        scratch_shapes=[pltpu.VMEM((D_MODEL, D_EXPERT), BF16), pltpu.VMEM((D_MODEL, D_EXPERT), BF16),
                        pltpu.VMEM((D_EXPERT, D_MODEL), BF16)],
    )
    return pl.pallas_call(
        _expert_kernel,
        grid_spec=grid_spec,
        out_shape=jax.ShapeDtypeStruct((rows, half), jnp.uint32),
        compiler_params=_params("arbitrary"),
        name="moe_experts",
    )(block_expert, block_rows, xs, wg, wu, wd)


def _combine_kernel(x_ref, r0_ref, r1_ref, rg_ref, gf_ref, o_ref, *, final):
    rg = rg_ref[...]
    y = (x_ref[...] + rg[:, 0:1] * _unpack_bf16_pairs(r0_ref[...])
         + rg[:, 1:2] * _unpack_bf16_pairs(r1_ref[...]))
    o_ref[...] = _rms(y, gf_ref[...]) if final else y


def _moe_combine(x1, yg, rg, gfinal, final):
    T = x1.shape[0]
    tm = min(ROW_TILE, T)
    half = D_MODEL // 2
    row = lambda n: pl.BlockSpec((tm, n), lambda i: (i, 0))
    return pl.pallas_call(
        functools.partial(_combine_kernel, final=final),
        grid=(T // tm,),
        in_specs=[row(D_MODEL), row(half), pl.BlockSpec((tm, half), lambda i: (i + T // tm, 0)),
                  row(LANES), _full((1, D_MODEL))],
        out_specs=row(D_MODEL),
        out_shape=jax.ShapeDtypeStruct((T, D_MODEL), F32),
        compiler_params=_params("parallel"),
        name="moe_combine",
    )(x1, yg, yg, rg, gfinal)


def _sc_rows(n_rows):
    per_worker = n_rows // SC_WORKERS
    assert n_rows % SC_WORKERS == 0 and per_worker % SC_CHUNK == 0
    return per_worker, per_worker // SC_CHUNK


def _sc_dispatch(hp, dest0, dest1, rows):
    T, W = hp.shape
    per_w, n_sub = _sc_rows(T)
    mesh = plsc.VectorSubcoreMesh(core_axis_name="c", subcore_axis_name="s")

    @functools.partial(
        pl.kernel, mesh=mesh,
        out_type=jax.ShapeDtypeStruct((rows, W), hp.dtype),
        scratch_types=[pltpu.VMEM((n_sub, SC_CHUNK), jnp.int32), pltpu.VMEM((n_sub, SC_CHUNK), jnp.int32),
                       pltpu.VMEM((SC_CHUNK, W), hp.dtype)],
    )
    def k(hp_hbm, d0_hbm, d1_hbm, out_hbm, d0_v, d1_v, buf):
        wid = lax.axis_index("s") * SC_CORES + lax.axis_index("c")
        pltpu.sync_copy(d0_hbm.at[wid], d0_v)
        pltpu.sync_copy(d1_hbm.at[wid], d1_v)

        @pl.loop(0, n_sub)
        def _(j):
            pltpu.sync_copy(hp_hbm.at[pl.ds(wid * per_w + j * SC_CHUNK, SC_CHUNK)], buf)
            pltpu.sync_copy(buf, out_hbm.at[d0_v.at[j]])
            pltpu.sync_copy(buf, out_hbm.at[d1_v.at[j]])

    shp = (SC_WORKERS, n_sub, SC_CHUNK)
    return k(hp, dest0.reshape(shp), dest1.reshape(shp))


def _sc_gather(table, idx):
    M = idx.shape[0]
    W = table.shape[1]
    per_w, n_sub = _sc_rows(M)
    assert n_sub % 2 == 0
    mesh = plsc.VectorSubcoreMesh(core_axis_name="c", subcore_axis_name="s")

    @functools.partial(
        pl.kernel, mesh=mesh,
        out_type=jax.ShapeDtypeStruct((M, W), table.dtype),
        scratch_types=[pltpu.VMEM((n_sub, SC_CHUNK), jnp.int32),
                       pltpu.VMEM((SC_CHUNK, W), table.dtype), pltpu.VMEM((SC_CHUNK, W), table.dtype),
                       pltpu.SemaphoreType.DMA, pltpu.SemaphoreType.DMA],
    )
    def k(table_hbm, idx_hbm, out_hbm, idx_v, buf0, buf1, sem0, sem1):
        wid = lax.axis_index("s") * SC_CORES + lax.axis_index("c")
        pltpu.sync_copy(idx_hbm.at[wid], idx_v)
        bufs, sems = (buf0, buf1), (sem0, sem1)

        def gather(j, slot):
            return pltpu.make_async_copy(table_hbm.at[idx_v.at[j]], bufs[slot], sems[slot])

        gather(0, 0).start()

        @pl.loop(0, n_sub // 2)
        def _(jj):
            for slot in range(2):
                j = jj * 2 + slot

                @pl.when(j + 1 < n_sub)
                def _():
                    gather(j + 1, 1 - slot).start()

                gather(j, slot).wait()
                pltpu.sync_copy(bufs[slot], out_hbm.at[pl.ds(wid * per_w + j * SC_CHUNK, SC_CHUNK)])

    return k(table, idx.reshape(SC_WORKERS, n_sub, SC_CHUNK))


def _moe(x1, hp, ri, rg, cnt, layer, wg, wu, wd, gfinal, final):
    T = x1.shape[0]
    bm = MOE_ROWS
    experts = jnp.arange(N_EXPERTS, dtype=jnp.int32)
    counts = cnt[0, :N_EXPERTS]
    padded = (counts + bm - 1) // bm * bm
    pad_ends = jnp.cumsum(padded)
    pad_starts = pad_ends - padded
    dest2 = ri[:, 2:4] + jnp.sum(jnp.where(ri[:, 0:2, None] > experts, padded, 0), axis=-1)
    n_blocks = (2 * T) // bm + N_EXPERTS
    block_start = jnp.arange(n_blocks, dtype=jnp.int32) * bm
    block_expert = jnp.minimum(jnp.sum((pad_ends[None, :] <= block_start[:, None]).astype(jnp.int32), axis=1),
                               N_EXPERTS - 1)
    block_rows = jnp.clip(counts[block_expert] - (block_start - pad_starts[block_expert]), 0, bm)
    xs = _sc_dispatch(hp, dest2[:, 0], dest2[:, 1], n_blocks * bm)
    y_rows = _expert_blocks(xs, block_expert, block_rows, layer, wg, wu, wd)
    yg = _sc_gather(y_rows, jnp.concatenate([dest2[:, 0], dest2[:, 1]]))
    return _moe_combine(x1, yg, rg, gfinal, final)


def _ssd_pre_kernel(x_ref, g_ref, w_ref, z_ref, xbc_ref, dt_ref):
    xn = _rms(x_ref[...], g_ref[...]).astype(BF16)
    step = 512
    for c in range(0, D_INNER, step):
        z_ref[:, c:c + step] = _dot(xn, w_ref[:, c:c + step]).astype(BF16)
    for c in range(0, XBC_DIM, step):
        xbc_ref[:, c:c + step] = _dot(xn, w_ref[:, D_INNER + c:D_INNER + c + step]).astype(BF16)
    dt_ref[...] = _dot(xn, w_ref[:, D_INNER + XBC_DIM:])


def _ssd_pre(x2, g, w_cat):
    T = x2.shape[0]
    tm = min(ROW_TILE, T)
    row = lambda n: pl.BlockSpec((tm, n), lambda i: (i, 0))
    return pl.pallas_call(
        _ssd_pre_kernel,
        grid=(T // tm,),
        in_specs=[row(D_MODEL), _full((1, D_MODEL)), _full(w_cat.shape)],
        out_specs=[row(D_INNER), row(XBC_DIM), row(LANES)],
        out_shape=[jax.ShapeDtypeStruct((T, D_INNER), BF16), jax.ShapeDtypeStruct((T, XBC_DIM), BF16),
                   jax.ShapeDtypeStruct((T, LANES), F32)],
        compiler_params=_params("parallel"),
        name="ssd_pre",
    )(x2, g, w_cat)


def _pair_expand(v, j, low):
    return jnp.where(low, v[:, 2 * j:2 * j + 1], v[:, 2 * j + 1:2 * j + 2])


def _ssd_scan_kernel(xbc_ref, dt_ref, cw_ref, cb_ref, dtb_ref, alog_ref, dsk_ref, y_ref,
                     ubuf, abuf, state):
    Q = CHUNK
    c = pl.program_id(1)

    @pl.when(c == 0)
    def _():
        ubuf[0:8, :] = jnp.zeros((8, XBC_DIM), F32)
        state[...] = jnp.zeros(state.shape, F32)

    ubuf[8:8 + Q, :] = xbc_ref[0].astype(F32)
    step = 512
    for cc in range(0, XBC_DIM, step):
        sl = slice(cc, cc + step)
        acc = cb_ref[:, sl] + cw_ref[3:4, sl] * ubuf[8:8 + Q, sl]
        for kk in range(D_CONV - 1):
            acc = acc + cw_ref[kk:kk + 1, sl] * ubuf[5 + kk:5 + kk + Q, sl]
        abuf[:, sl] = acc * _sigmoid(acc)
    ubuf[0:8, :] = ubuf[Q:Q + 8, :]

    x_dt = dt_ref[0] + dtb_ref[...]
    dt = jnp.maximum(x_dt, 0.0) + jnp.log(1.0 + jnp.exp(-jnp.abs(x_dt)))
    da = dt * (-jnp.exp(alog_ref[...]))
    ri = lax.broadcasted_iota(jnp.int32, (Q, Q), 0)
    ci = lax.broadcasted_iota(jnp.int32, (Q, Q), 1)
    causal = ci <= ri
    tril = jnp.where(causal, 1.0, 0.0).astype(BF16)
    d_hi = da.astype(BF16)
    r1 = da - d_hi.astype(F32)
    d_mid = r1.astype(BF16)
    d_lo = (r1 - d_mid.astype(F32)).astype(BF16)
    cs = _dot(tril, d_hi) + _dot(tril, d_mid) + _dot(tril, d_lo)
    cs_t = jnp.transpose(cs)
    cs_last = cs[Q - 1:Q, :]
    ecs = jnp.exp(cs)
    to_end = jnp.exp(cs_last - cs)
    dec_last = jnp.exp(cs_last)
    low = lax.broadcasted_iota(jnp.int32, (Q, LANES), 1) < SSD_HEADDIM
    low1 = low[0:1, :]
    hg = SSD_HEADS // SSD_GROUPS
    gw = hg * SSD_HEADDIM
    for g in range(SSD_GROUPS):
        b_g = abuf[:, D_INNER + g * D_STATE:D_INNER + (g + 1) * D_STATE]
        c_g = abuf[:, D_INNER + SSD_GROUPS * D_STATE + g * D_STATE:
                   D_INNER + SSD_GROUPS * D_STATE + (g + 1) * D_STATE].astype(BF16)
        cb = _dot_nt(c_g, b_g.astype(BF16))
        b_t = jnp.transpose(b_g).astype(BF16)
        st = state[g]
        y_inter = _dot(c_g, st.astype(BF16))
        xw_parts, dl_parts = [], []
        for p in range(hg // 2):
            j = g * (hg // 2) + p
            lanes = slice(j * LANES, (j + 1) * LANES)
            x_p = abuf[:, lanes]
            xdt = x_p * _pair_expand(dt, j, low)
            ys = []
            for e in range(2):
                h = 2 * j + e
                seg = cs[:, h:h + 1] - cs_t[h:h + 1, :]
                lmat = jnp.exp(jnp.where(causal, seg, NEG_INF))
                ys.append(_dot((cb * lmat).astype(BF16), xdt.astype(BF16)))
            y_p = (jnp.where(low, ys[0], ys[1])
                   + y_inter[:, p * LANES:(p + 1) * LANES] * _pair_expand(ecs, j, low)
                   + dsk_ref[:, lanes] * x_p)
            y_ref[0, :, lanes] = y_p.astype(y_ref.dtype)
            xw_parts.append((xdt * _pair_expand(to_end, j, low)).astype(BF16))
            dl_parts.append(_pair_expand(dec_last, j, low1))
        xw = jnp.concatenate(xw_parts, axis=1)
        dl = jnp.concatenate(dl_parts, axis=1)
        state[g] = st * dl + _dot(b_t, xw)
    del gw


def _ssd_scan(xbc, dt, conv_w, conv_b, dt_bias, a_log, d_skip):
    B, S, _ = xbc.shape
    Q = CHUNK
    hg = SSD_HEADS // SSD_GROUPS
    return pl.pallas_call(
        _ssd_scan_kernel,
        grid=(B, S // Q),
        in_specs=[pl.BlockSpec((1, Q, XBC_DIM), lambda b, c: (b, c, 0)),
                  pl.BlockSpec((1, Q, LANES), lambda b, c: (b, c, 0)),
                  _full((D_CONV, XBC_DIM)), _full((1, XBC_DIM)), _full((1, LANES)), _full((1, LANES)),
                  _full((1, D_INNER))],
        out_specs=pl.BlockSpec((1, Q, D_INNER), lambda b, c: (b, c, 0)),
        out_shape=jax.ShapeDtypeStruct((B, S, D_INNER), BF16),
        scratch_shapes=[pltpu.VMEM((Q + 8, XBC_DIM), F32), pltpu.VMEM((Q, XBC_DIM), F32),
                        pltpu.VMEM((SSD_GROUPS, D_STATE, hg * SSD_HEADDIM), F32)],
        compiler_params=_params("parallel", "arbitrary"),
        name="ssd_scan",
    )(xbc, dt, conv_w, conv_b, dt_bias, a_log, d_skip)


def _ssd_post_kernel(y_ref, z_ref, gn_ref, x_ref, wo_ref, *tail_refs):
    gsz = D_INNER // SSD_GROUPS
    out = None
    for g in range(SSD_GROUPS):
        sl = slice(g * gsz, (g + 1) * gsz)
        z = z_ref[:, sl].astype(F32)
        yz = y_ref[:, sl].astype(F32) * (z * _sigmoid(z))
        part = _dot(_rms(yz, gn_ref[:, sl]).astype(BF16), wo_ref[sl, :])
        out = part if out is None else out + part
    _route_tail(x_ref[...] + out, *tail_refs)


def _ssd_post(y, z, gn, x2, wo, tail):
    T = x2.shape[0]
    tm = min(ROW_TILE, T)
    row = lambda n: pl.BlockSpec((tm, n), lambda i: (i, 0))
    t_in, t_out = _tail_specs(tm)
    return pl.pallas_call(
        _ssd_post_kernel,
        grid=(T // tm,),
        in_specs=[row(D_INNER), row(D_INNER), _full((1, D_INNER)), row(D_MODEL), _full(wo.shape)] + t_in,
        out_specs=t_out,
        out_shape=_tail_shapes(T),
        scratch_shapes=[pltpu.VMEM((1, LANES), F32)],
        compiler_params=_params("arbitrary"),
        name="ssd_post",
    )(y, z, gn, x2, wo, *tail)


def _rope_tables(positions):
    pos = positions.reshape(-1, 1).astype(F32)
    lane = jnp.arange(LANES)

    def tables(rot, lead, period):
        half = rot // 2
        off = lane % period - lead
        rotary = (off >= 0) & (off < rot)
        inv = ROPE_THETA ** (-(2 * (off % half)).astype(F32) / rot)
        ang = pos * jnp.where(rotary, inv, 0.0)[None, :]
        sign = jnp.where(rotary, jnp.where(off < half, -1.0, 1.0), 0.0)
        return [jnp.cos(ang), jnp.sin(ang) * sign[None, :]]

    return tables(ROPE_DIM_A, 0, HEAD_DIM_A) + tables(MLA_ROPE, MLA_NOPE, LANES)


def _pad_cols(w, n):
    return jnp.pad(w, ((0, 0), (0, n - w.shape[1])))


def _head_pad(w, width):
    K = w.shape[0]
    w = w.reshape(K, -1, width)
    return jnp.pad(w, ((0, 0), (0, 0), (0, LANES - width))).reshape(K, -1)


def _router_weights(w_group, b_group, w_router, b_router):
    w = _pad_cols(jnp.concatenate([w_router, w_group], axis=1), LANES)
    b = _pad_cols(jnp.concatenate([b_router, b_group])[None, :], LANES)
    hi = w.astype(BF16)
    lo = (w - hi.astype(F32)).astype(BF16)
    return hi, lo, b


def kernel(x, positions, attn_norm, w_in_attn, mla_q_norm, w_uq, mla_kv_norm, w_ukv, w_out_attn,
           ssd_norm, w_in_ssd, conv_w, conv_b, dt_bias, a_log, d_skip, gate_norm, w_out_ssd,
           moe_norm, w_group, b_group, w_router, b_router, w_gate, w_up, w_down, final_norm):
    B, S, D = x.shape
    T = B * S
    x2 = x.reshape(T, D)
    tabs = _rope_tables(positions)

    def tail_params(layer):
        hi, lo, b = _router_weights(w_group[layer], b_group[layer], w_router[layer], b_router[layer])
        return [moe_norm[layer][None, :], hi, lo, b]

    w_in = w_in_attn[0]
    kr_off = 3 * MIX_A + MLA_Q_RANK + MLA_KV_RANK
    kr_cols = jnp.pad(w_in[:, kr_off:], ((0, 0), (MLA_NOPE, LANES - MLA_NOPE - MLA_ROPE)))
    w_cat = jnp.concatenate([w_in[:, :kr_off], kr_cols], axis=1).astype(BF16)
    wuq = _head_pad(w_uq[0], MLA_NOPE + MLA_ROPE).astype(BF16)
    ukv = w_ukv[0].reshape(MLA_KV_RANK, MLA_HEADS, MLA_NOPE + MLA_V)
    wuk = _head_pad(ukv[:, :, :MLA_NOPE].reshape(MLA_KV_RANK, -1), MLA_NOPE).astype(BF16)
    wuv = _head_pad(ukv[:, :, MLA_NOPE:].reshape(MLA_KV_RANK, -1), MLA_V).astype(BF16)
    pre = _attn_pre(x2, attn_norm[0][None, :], w_cat, mla_q_norm[0][None, :], wuq,
                    mla_kv_norm[0][None, :], wuk, wuv, tabs, S)
    qm, km, vm = pre[9:]
    shp = lambda t: t.reshape(B, S, t.shape[-1])
    qkv = [[t.reshape(B, 1, S, MIX_A) for t in pre[0:3]], pre[3:6], pre[6:9]]
    prev = None
    for n, d in enumerate(DILATIONS):
        prev = _dilated_branch(*qkv[n], d, prev, n == len(DILATIONS) - 1)
    ob = _mla_attention(shp(qm), shp(km), shp(vm)).reshape(T, MIX_B)
    x1, h, ri, rg, cnt = _attn_post(prev[0], ob, x2, w_out_attn[0].astype(BF16), tail_params(0))
    x2 = _moe(x1, h, ri, rg, cnt, 0, w_gate, w_up, w_down, final_norm[None, :], False)

    w_cat = _pad_cols(w_in_ssd[0], D_INNER + XBC_DIM + LANES).astype(BF16)
    z, xbc, dt = _ssd_pre(x2, ssd_norm[0][None, :], w_cat)
    y = _ssd_scan(xbc.reshape(B, S, XBC_DIM), dt.reshape(B, S, LANES), conv_w[0], conv_b[0][None, :],
                  _pad_cols(dt_bias[0][None, :], LANES), _pad_cols(a_log[0][None, :], LANES),
                  jnp.repeat(d_skip[0], SSD_HEADDIM)[None, :])
    x1, h, ri, rg, cnt = _ssd_post(y.reshape(T, D_INNER), z, gate_norm[0][None, :], x2,
                                   w_out_ssd[0].astype(BF16), tail_params(1))
    out = _moe(x1, h, ri, rg, cnt, 1, w_gate, w_up, w_down, final_norm[None, :], True)
    return out.reshape(B, S, D)
```

```python
import functools

import jax
import jax.numpy as jnp
from jax import lax
from jax.experimental import pallas as pl
from jax.experimental.pallas import tpu as pltpu
from jax.experimental.pallas import tpu_sc as plsc

F32 = jnp.float32
BF16 = jnp.bfloat16

D_MODEL = 1024
HEADS_A = 8
HEAD_DIM_A = 64
ROPE_DIM_A = 16
DILATIONS = (1, 4, 16)
SPAN = 128
DIL_ROWS = 2048
MLA_HEADS = 8
MLA_NOPE = 64
MLA_ROPE = 32
MLA_V = 64
MLA_Q_RANK = 256
MLA_KV_RANK = 128
ROPE_THETA = 500000.0
MIX_A = HEADS_A * HEAD_DIM_A
MIX_B = MLA_HEADS * MLA_V
D_INNER = 2048
SSD_HEADDIM = 64
SSD_HEADS = 32
SSD_GROUPS = 4
D_STATE = 128
D_CONV = 4
CHUNK = 256
XBC_DIM = D_INNER + 2 * SSD_GROUPS * D_STATE
N_GROUPS = 4
EXPERTS_PER_GROUP = 8
N_EXPERTS = 32
D_EXPERT = 512
NORM_EPS = 1e-6

LANES = 128
VMEM_LIMIT = 56 * 1024 * 1024
ROW_TILE = 512
MOE_ROWS = 512
SC_CORES = 2
SC_WORKERS = 32
SC_CHUNK = 64
MLA_TQ = 2048
MLA_TK = 512
MLA_UNROLL = 2
NEG_INF = float("-inf")
LOG2E = 1.4426950408889634


def _params(*sem):
    return pltpu.CompilerParams(dimension_semantics=sem, vmem_limit_bytes=VMEM_LIMIT)


def _dot(a, b):
    return jnp.dot(a, b, preferred_element_type=F32)


def _dot_nt(a, b):
    return lax.dot_general(a, b, (((1,), (1,)), ((), ())), preferred_element_type=F32)


def _rms(x, g):
    return x * lax.rsqrt(jnp.mean(x * x, axis=-1, keepdims=True) + NORM_EPS) * g


def _sigmoid(x):
    return 1.0 / (1.0 + jnp.exp(-x))


def _rope(t, c, s, half, first):
    partner = jnp.where(first, pltpu.roll(t, LANES - half, 1), pltpu.roll(t, half, 1))
    return t * c + partner * s


def _pack_bf16_pairs(x):
    n = x.shape[1] // 2
    bits = lax.bitcast_convert_type(x.astype(BF16).astype(F32), jnp.uint32)
    return (bits[:, :n] >> 16) | (bits[:, n:] & jnp.uint32(0xFFFF0000))


def _unpack_bf16_pairs(w):
    lo = lax.bitcast_convert_type(w << 16, F32)
    hi = lax.bitcast_convert_type(w & jnp.uint32(0xFFFF0000), F32)
    return jnp.concatenate([lo, hi], axis=1)


def _full(shape):
    return pl.BlockSpec(shape, lambda *_: (0,) * len(shape))


def _attn_pre_kernel(x_ref, g_ref, w_ref, qn_ref, wuq_ref, kvn_ref, wuk_ref, wuv_ref,
                     ca_ref, sa_ref, cb_ref, sb_ref,
                     qa_ref, ka_ref, va_ref, q4_ref, k4_ref, v4_ref, q16_ref, k16_ref, v16_ref,
                     qm_ref, km_ref, vm_ref, lay):
    xn = _rms(x_ref[...], g_ref[...]).astype(BF16)
    ca, sa, cb, sb = ca_ref[...], sa_ref[...], cb_ref[...], sb_ref[...]
    half_a = ROPE_DIM_A // 2
    half_b = MLA_ROPE // 2
    lane = lax.broadcasted_iota(jnp.int32, ca.shape, 1)
    first_a = lane % HEAD_DIM_A < half_a
    first_b = (lane >= MLA_NOPE) & (lane < MLA_NOPE + half_b)

    q = _dot(xn, w_ref[:, 0:MIX_A]) * (HEAD_DIM_A ** -0.5)
    k = _dot(xn, w_ref[:, MIX_A:2 * MIX_A])
    v = _dot(xn, w_ref[:, 2 * MIX_A:3 * MIX_A])
    for c in range(MIX_A // LANES):
        sl = slice(c * LANES, (c + 1) * LANES)
        lay[0, c] = _rope(q[:, sl], ca, sa, half_a, first_a)
        lay[1, c] = _rope(k[:, sl], ca, sa, half_a, first_a)
        lay[2, c] = v[:, sl]
    tm = x_ref.shape[0]
    for a, (nat, by4, by16) in enumerate(((qa_ref, q4_ref, q16_ref), (ka_ref, k4_ref, k16_ref),
                                          (va_ref, v4_ref, v16_ref))):
        for c in range(MIX_A // LANES):
            sl = slice(c * LANES, (c + 1) * LANES)
            nat[:, sl] = lay[a, c].astype(BF16)
            for d, ref in ((4, by4), (16, by16)):
                for r in range(d):
                    ref[0, r, :, sl] = lay[a, c, pl.ds(r, tm // d, stride=d), :].astype(BF16)

    o = 3 * MIX_A
    cq = _dot(xn, w_ref[:, o:o + MLA_Q_RANK])
    ckv = _dot(xn, w_ref[:, o + MLA_Q_RANK:o + MLA_Q_RANK + MLA_KV_RANK])
    kr = _dot(xn, w_ref[:, o + MLA_Q_RANK + MLA_KV_RANK:o + MLA_Q_RANK + MLA_KV_RANK + LANES])
    cqn = _rms(cq, qn_ref[...]).astype(BF16)
    kvn = _rms(ckv, kvn_ref[...]).astype(BF16)
    krr = _rope(kr, cb, sb, half_b, first_b)
    qm = _dot(cqn, wuq_ref[...]) * ((MLA_NOPE + MLA_ROPE) ** -0.5 * LOG2E)
    km = _dot(kvn, wuk_ref[...])
    for h in range(MLA_HEADS):
        sl = slice(h * LANES, (h + 1) * LANES)
        qm_ref[:, sl] = _rope(qm[:, sl], cb, sb, half_b, first_b).astype(BF16)
        km_ref[:, sl] = (km[:, sl] + krr).astype(BF16)
    vm = _dot(kvn, wuv_ref[...])
    low = lax.broadcasted_iota(jnp.int32, vm.shape, 1) % LANES < MLA_V
    vm_ref[...] = jnp.where(low, vm, 1.0).astype(BF16)


def _attn_pre(x2, g, w_cat, qn, wuq, kvn, wuk, wuv, tabs, seq):
    T = x2.shape[0]
    tm = min(ROW_TILE, T)
    nb = seq // tm
    row = lambda n: pl.BlockSpec((tm, n), lambda i: (i, 0))
    wide = MLA_HEADS * LANES
    by_residue = [pl.BlockSpec((1, d, tm // d, MIX_A), lambda i: (i // nb, 0, i % nb, 0))
                  for d in DILATIONS[1:]]
    residue_shapes = [jax.ShapeDtypeStruct((T // seq, d, seq // d, MIX_A), BF16) for d in DILATIONS[1:]]
    return pl.pallas_call(
        _attn_pre_kernel,
        grid=(T // tm,),
        in_specs=[row(D_MODEL), _full((1, D_MODEL)), _full(w_cat.shape), _full((1, MLA_Q_RANK)),
                  _full(wuq.shape), _full((1, MLA_KV_RANK)), _full(wuk.shape), _full(wuv.shape)]
                 + [row(LANES)] * 4,
        out_specs=[row(MIX_A)] * 3 + [by_residue[0]] * 3 + [by_residue[1]] * 3 + [row(wide)] * 3,
        out_shape=[jax.ShapeDtypeStruct((T, MIX_A), BF16)] * 3 + [residue_shapes[0]] * 3
                  + [residue_shapes[1]] * 3 + [jax.ShapeDtypeStruct((T, wide), BF16)] * 3,
        scratch_shapes=[pltpu.VMEM((3, MIX_A // LANES, tm, LANES), F32)],
        compiler_params=_params("parallel"),
        name="attn_pre",
    )(x2, g, w_cat, qn, wuq, kvn, wuk, wuv, *tabs)


def _dilated_kernel(*refs, d, tq, carry, last):
    refs = list(refs)
    q_ref, kc_ref, kp_ref, vc_ref, vp_ref = refs[:5]
    op_ref, lp_ref = refs[5:7] if carry else (None, None)
    kbuf, vbuf = refs[-2:]
    o_ref = refs[7] if carry else refs[5]
    lse_ref = None if last else refs[-3]
    j = pl.program_id(2)
    for r in range(d):
        kbuf[r, 0:SPAN, :] = kp_ref[0, r]
        kbuf[r, SPAN:, :] = kc_ref[0, r]
        vbuf[r, 0:SPAN, :] = vp_ref[0, r]
        vbuf[r, SPAN:, :] = vc_ref[0, r]
    low = lax.broadcasted_iota(jnp.int32, (SPAN, LANES), 1) < HEAD_DIM_A
    qi = lax.broadcasted_iota(jnp.int32, (SPAN, 2 * SPAN), 0)
    ki = lax.broadcasted_iota(jnp.int32, (SPAN, 2 * SPAN), 1)
    band = (ki >= qi) & (ki <= qi + SPAN)
    first_valid = band & ((ki >= SPAN) | (j > 0))
    for r in range(d):
        for sub in range(tq // SPAN):
            valid = first_valid if sub == 0 else band
            q2 = q_ref[0, r, sub * SPAN:(sub + 1) * SPAN, :]
            kk = kbuf[r, sub * SPAN:(sub + 2) * SPAN, :]
            vv = vbuf[r, sub * SPAN:(sub + 2) * SPAN, :]
            os_, ls_ = [], []
            for e in range(2):
                qe = jnp.where(low if e == 0 else jnp.logical_not(low), q2, jnp.zeros_like(q2))
                s = jnp.where(valid, _dot_nt(qe, kk), NEG_INF)
                m = jnp.max(s, axis=-1, keepdims=True)
                pr = jnp.exp(s - m)
                l = jnp.sum(pr, axis=-1, keepdims=True)
                os_.append(_dot(pr.astype(BF16), vv) / l)
                ls_.append(m + jnp.log(l))
            o2 = jnp.where(low, os_[0], os_[1])
            l2 = jnp.where(low, ls_[0], ls_[1])
            rows = pl.ds(sub * SPAN, SPAN) if d == 1 else pl.ds(d * sub * SPAN + r, SPAN, stride=d)
            if carry:
                lp = lp_ref[0, 0, rows, :]
                mx = jnp.maximum(lp, l2)
                wp = jnp.exp(lp - mx)
                wc = jnp.exp(l2 - mx)
                den = wp + wc
                o2 = (wp * op_ref[0, 0, rows, :] + wc * o2) / den
                l2 = mx + jnp.log(den)
            o_ref[0, 0, rows, :] = o2
            if not last:
                lse_ref[0, 0, rows, :] = l2


def _dilated_branch(q, k, v, d, prev, last):
    B, _, L, _ = q.shape
    pairs = MIX_A // LANES
    tq = DIL_ROWS // d
    cur = pl.BlockSpec((1, d, tq, LANES), lambda b, p, j: (b, 0, j, p))
    prv = pl.BlockSpec((1, d, SPAN, LANES), lambda b, p, j: (b, 0, jnp.maximum(j * (tq // SPAN) - 1, 0), p))
    nat = pl.BlockSpec((1, 1, DIL_ROWS, LANES), lambda b, p, j: (b, p, j, 0))
    nat_shape = jax.ShapeDtypeStruct((B, pairs, d * L, LANES), F32)
    carry = prev is not None
    outs = pl.pallas_call(
        functools.partial(_dilated_kernel, d=d, tq=tq, carry=carry, last=last),
        grid=(B, pairs, L // tq),
        in_specs=[cur, cur, prv, cur, prv] + ([nat, nat] if carry else []),
        out_specs=[nat] if last else [nat, nat],
        out_shape=[nat_shape] if last else [nat_shape, nat_shape],
        scratch_shapes=[pltpu.VMEM((d, tq + SPAN, LANES), BF16), pltpu.VMEM((d, tq + SPAN, LANES), BF16)],
        compiler_params=_params("parallel", "parallel", "parallel"),
        name=f"dilated_d{d}",
    )(q, k, k, v, v, *(prev if carry else ()))
    return outs


def _mla_kernel(q_ref, k_ref, v_ref, o_ref, m0, m1, a0, a1, *, tq, tk):
    qi = pl.program_id(2)
    ms, accs = (m0, m1), (a0, a1)
    for e in range(2):
        ms[e][...] = jnp.full(ms[e].shape, NEG_INF, F32)
        accs[e][...] = jnp.zeros(accs[e].shape, F32)

    def step(kv, diag):
        start = pl.multiple_of(kv * tk, tk)
        r0 = 0 if diag is None else diag * tk
        for e in range(2):
            lanes = slice(e * LANES, (e + 1) * LANES)
            s = _dot_nt(q_ref[0, r0:, lanes], k_ref[0, pl.ds(start, tk), lanes])
            if diag is not None:
                row = lax.broadcasted_iota(jnp.int32, s.shape, 0)
                col = lax.broadcasted_iota(jnp.int32, s.shape, 1)
                s = jnp.where(col <= row, s, NEG_INF)
            m_prev = ms[e][r0:, :]
            m_new = jnp.maximum(m_prev, jnp.max(s, axis=-1, keepdims=True))
            alpha = jnp.exp2(m_prev - m_new)
            pr = jnp.concatenate([jnp.exp2(s[:, c * LANES:(c + 1) * LANES] - m_new)
                                  for c in range(tk // LANES)], axis=1).astype(BF16)
            accs[e][r0:, :] = alpha * accs[e][r0:, :] + _dot(pr, v_ref[0, pl.ds(start, tk), lanes])
            ms[e][r0:, :] = m_new

    n_full = qi * (tq // tk)
    n_main = n_full // MLA_UNROLL

    def body(i, c):
        for u in range(MLA_UNROLL):
            step(MLA_UNROLL * i + u, None)
        return c

    def single(i, c):
        step(i, None)
        return c

    lax.fori_loop(0, n_main, body, 0)
    lax.fori_loop(n_main * MLA_UNROLL, n_full, single, 0)

    for t in range(tq // tk):
        step(n_full + t, t)
    low = lax.broadcasted_iota(jnp.int32, (tq, LANES), 1) < MLA_V
    r0 = a0[...] / pltpu.roll(a0[...], MLA_V, 1)
    r1 = a1[...] / pltpu.roll(a1[...], MLA_V, 1)
    o_ref[0] = jnp.where(low, r0, pltpu.roll(r1, MLA_V, 1)).astype(BF16)


def _mla_attention(qm, km, vm):
    B, S, _ = qm.shape
    tq = min(MLA_TQ, S)
    tk = min(MLA_TK, tq)
    pairs = MLA_HEADS // 2
    return pl.pallas_call(
        functools.partial(_mla_kernel, tq=tq, tk=tk),
        grid=(B, pairs, S // tq),
        in_specs=[pl.BlockSpec((1, tq, 2 * LANES), lambda b, p, i: (b, i, p)),
                  pl.BlockSpec((1, S, 2 * LANES), lambda b, p, i: (b, 0, p)),
                  pl.BlockSpec((1, S, 2 * LANES), lambda b, p, i: (b, 0, p))],
        out_specs=pl.BlockSpec((1, tq, LANES), lambda b, p, i: (b, i, p)),
        out_shape=jax.ShapeDtypeStruct((B, S, MIX_B), BF16),
        scratch_shapes=[pltpu.VMEM((tq, LANES), F32)] * 4,
        compiler_params=_params("parallel", "parallel", "arbitrary"),
        name="mla_flash",
    )(qm, km, vm)


def _route_tail(x1, gm_ref, wr_ref, br_ref, x1_ref, h_ref, ri_ref, rg_ref, cnt_ref, cnt_sc):
    x1_ref[...] = x1
    hn = _rms(x1, gm_ref[...])
    h_hi = hn.astype(BF16)
    h_ref[...] = _pack_bf16_pairs(hn)
    h_lo = (hn - h_hi.astype(F32)).astype(BF16)
    both = _dot(h_hi, wr_ref[...])
    logits = both[:, :LANES] + both[:, LANES:] + _dot(h_lo, wr_ref[:, 0:LANES]) + br_ref[...]
    tm = logits.shape[0]
    lane = lax.broadcasted_iota(jnp.int32, (tm, LANES), 1)
    lanef = lane.astype(F32)
    big = float(LANES)
    is_g = (lane >= N_EXPERTS) & (lane < N_EXPERTS + N_GROUPS)
    gl = jnp.where(is_g, logits, NEG_INF)
    gmax = jnp.max(gl, axis=-1, keepdims=True)
    gsum = jnp.sum(jnp.exp(gl - gmax), axis=-1, keepdims=True)
    g_val = 1.0 / gsum
    g_idx = jnp.min(jnp.where(gl == gmax, lanef, big), axis=-1, keepdims=True) - float(N_EXPERTS)
    in_grp = (lane < N_EXPERTS) & ((lane // EXPERTS_PER_GROUP).astype(F32) == g_idx)
    el = jnp.where(in_grp, logits, NEG_INF)
    e1 = jnp.max(el, axis=-1, keepdims=True)
    i1 = jnp.min(jnp.where(el == e1, lanef, big), axis=-1, keepdims=True)
    el2 = jnp.where(lanef == i1, NEG_INF, el)
    e2 = jnp.max(el2, axis=-1, keepdims=True)
    i2 = jnp.min(jnp.where(el2 == e2, lanef, big), axis=-1, keepdims=True)
    t = jnp.exp(e2 - e1)
    p1 = 1.0 / (1.0 + t)
    rg_ref[...] = jnp.where(lane == 0, p1 * g_val, jnp.where(lane == 1, t * p1 * g_val, 0.0))
    @pl.when(pl.program_id(0) == 0)
    def _():
        cnt_sc[...] = jnp.zeros(cnt_sc.shape, F32)

    oh1 = lanef == i1
    oh2 = lanef == i2
    oh = jnp.where(oh1 | oh2, 1.0, 0.0)
    rr = lax.broadcasted_iota(jnp.int32, (tm, tm), 0)
    cc = lax.broadcasted_iota(jnp.int32, (tm, tm), 1)
    strict = jnp.where(cc < rr, 1.0, 0.0).astype(BF16)
    before = _dot(strict, oh.astype(BF16)) + cnt_sc[...]
    rank1 = jnp.sum(jnp.where(oh1, before, 0.0), axis=-1, keepdims=True)
    rank2 = jnp.sum(jnp.where(oh2, before, 0.0), axis=-1, keepdims=True)
    cnt_sc[...] = cnt_sc[...] + jnp.sum(oh, axis=0, keepdims=True)
    cnt_ref[...] = jnp.broadcast_to(cnt_sc[...], cnt_ref.shape).astype(jnp.int32)
    ri_ref[...] = jnp.where(lane == 0, i1, jnp.where(lane == 1, i2, jnp.where(
        lane == 2, rank1, jnp.where(lane == 3, rank2, 0.0)))).astype(jnp.int32)


def _tail_specs(tm):
    row = lambda n: pl.BlockSpec((tm, n), lambda i: (i, 0))
    in_specs = [_full((1, D_MODEL)), _full((D_MODEL, 2 * LANES)), _full((1, LANES))]
    out_specs = [row(D_MODEL), row(D_MODEL // 2), row(LANES), row(LANES), _full((8, LANES))]
    return in_specs, out_specs


def _tail_shapes(T):
    return [jax.ShapeDtypeStruct((T, D_MODEL), F32), jax.ShapeDtypeStruct((T, D_MODEL // 2), jnp.uint32),
            jax.ShapeDtypeStruct((T, LANES), jnp.int32), jax.ShapeDtypeStruct((T, LANES), F32),
            jax.ShapeDtypeStruct((8, LANES), jnp.int32)]


def _attn_post_kernel(oa_ref, ob_ref, x_ref, wo_ref, *tail_refs):
    oa = jnp.concatenate([oa_ref[0, p] for p in range(MIX_A // LANES)], axis=1).astype(BF16)
    y = _dot(oa, wo_ref[0:MIX_A, :]) + _dot(ob_ref[...], wo_ref[MIX_A:, :])
    _route_tail(x_ref[...] + y, *tail_refs)


def _attn_post(oa, ob, x2, wo, tail):
    T = x2.shape[0]
    tm = min(ROW_TILE, T)
    nb = oa.shape[2] // tm
    row = lambda n: pl.BlockSpec((tm, n), lambda i: (i, 0))
    t_in, t_out = _tail_specs(tm)
    return pl.pallas_call(
        _attn_post_kernel,
        grid=(T // tm,),
        in_specs=[pl.BlockSpec((1, MIX_A // LANES, tm, LANES), lambda i: (i // nb, 0, i % nb, 0)),
                  row(MIX_B), row(D_MODEL), _full(wo.shape)] + t_in,
        out_specs=t_out,
        out_shape=_tail_shapes(T),
        scratch_shapes=[pltpu.VMEM((1, LANES), F32)],
        compiler_params=_params("arbitrary"),
        name="attn_post",
    )(oa, ob, x2, wo, *tail)


def _expert_kernel(be_ref, nr_ref, xs_ref, wg_ref, wu_ref, wd_ref, y_ref, wg_bf, wu_bf, wd_bf):
    i = pl.program_id(0)

    @pl.when((i == 0) | (be_ref[i] != be_ref[jnp.maximum(i - 1, 0)]))
    def _():
        wg_bf[...] = wg_ref[0].astype(BF16)
        wu_bf[...] = wu_ref[0].astype(BF16)
        wd_bf[...] = wd_ref[0].astype(BF16)

    @pl.when(nr_ref[i] > 0)
    def _():
        half = xs_ref.shape[0] // 2
        for part in range(2):
            rows = slice(part * half, (part + 1) * half)
            live = lax.broadcasted_iota(jnp.int32, (half, xs_ref.shape[1]), 0) + part * half < nr_ref[i]
            xb = _unpack_bf16_pairs(jnp.where(live, xs_ref[rows, :], jnp.uint32(0))).astype(BF16)
            g = _dot(xb, wg_bf[...])
            u = _dot(xb, wu_bf[...])
            hid = (g * _sigmoid(g) * u).astype(BF16)
            y_ref[rows, :] = _pack_bf16_pairs(_dot(hid, wd_bf[...]))

    @pl.when(nr_ref[i] == 0)
    def _():
        y_ref[...] = jnp.zeros(y_ref.shape, y_ref.dtype)


def _expert_blocks(xs, block_expert, block_rows, layer, wg, wu, wd):
    rows = xs.shape[0]
    bm = MOE_ROWS
    half = D_MODEL // 2
    grid_spec = pltpu.PrefetchScalarGridSpec(
        num_scalar_prefetch=2,
        grid=(rows // bm,),
        in_specs=[pl.BlockSpec((bm, half), lambda i, be, nr: (i, 0)),
                  pl.BlockSpec((None, 1, D_MODEL, D_EXPERT), lambda i, be, nr: (layer, be[i], 0, 0)),
                  pl.BlockSpec((None, 1, D_MODEL, D_EXPERT), lambda i, be, nr: (layer, be[i], 0, 0)),
                  pl.BlockSpec((None, 1, D_EXPERT, D_MODEL), lambda i, be, nr: (layer, be[i], 0, 0))],
        out_specs=pl.BlockSpec((bm, half), lambda i, be, nr: (i, 0)),
        scratch_shapes=[pltpu.VMEM((D_MODEL, D_EXPERT), BF16), pltpu.VMEM((D_MODEL, D_EXPERT), BF16),
                        pltpu.VMEM((D_EXPERT, D_MODEL), BF16)],
    )
    return pl.pallas_call(
        _expert_kernel,
        grid_spec=grid_spec,
        out_shape=jax.ShapeDtypeStruct((rows, half), jnp.uint32),
        compiler_params=_params("arbitrary"),
        name="moe_experts",
    )(block_expert, block_rows, xs, wg, wu, wd)


def _combine_kernel(x_ref, r0_ref, r1_ref, rg_ref, gf_ref, o_ref, *, final):
    rg = rg_ref[...]
    y = (x_ref[...] + rg[:, 0:1] * _unpack_bf16_pairs(r0_ref[...])
         + rg[:, 1:2] * _unpack_bf16_pairs(r1_ref[...]))
    o_ref[...] = _rms(y, gf_ref[...]) if final else y


def _moe_combine(x1, yg, rg, gfinal, final):
    T = x1.shape[0]
    tm = min(ROW_TILE, T)
    half = D_MODEL // 2
    row = lambda n: pl.BlockSpec((tm, n), lambda i: (i, 0))
    return pl.pallas_call(
        functools.partial(_combine_kernel, final=final),
        grid=(T // tm,),
        in_specs=[row(D_MODEL), row(half), pl.BlockSpec((tm, half), lambda i: (i + T // tm, 0)),
                  row(LANES), _full((1, D_MODEL))],
        out_specs=row(D_MODEL),
        out_shape=jax.ShapeDtypeStruct((T, D_MODEL), F32),
        compiler_params=_params("parallel"),
        name="moe_combine",
    )(x1, yg, yg, rg, gfinal)


def _sc_rows(n_rows):
    per_worker = n_rows // SC_WORKERS
    assert n_rows % SC_WORKERS == 0 and per_worker % SC_CHUNK == 0
    return per_worker, per_worker // SC_CHUNK


def _sc_dispatch(hp, dest0, dest1, rows):
    T, W = hp.shape
    per_w, n_sub = _sc_rows(T)
    mesh = plsc.VectorSubcoreMesh(core_axis_name="c", subcore_axis_name="s")

    @functools.partial(
        pl.kernel, mesh=mesh,
        out_type=jax.ShapeDtypeStruct((rows, W), hp.dtype),
        scratch_types=[pltpu.VMEM((n_sub, SC_CHUNK), jnp.int32), pltpu.VMEM((n_sub, SC_CHUNK), jnp.int32),
                       pltpu.VMEM((SC_CHUNK, W), hp.dtype)],
    )
    def k(hp_hbm, d0_hbm, d1_hbm, out_hbm, d0_v, d1_v, buf):
        wid = lax.axis_index("s") * SC_CORES + lax.axis_index("c")
        pltpu.sync_copy(d0_hbm.at[wid], d0_v)
        pltpu.sync_copy(d1_hbm.at[wid], d1_v)

        @pl.loop(0, n_sub)
        def _(j):
            pltpu.sync_copy(hp_hbm.at[pl.ds(wid * per_w + j * SC_CHUNK, SC_CHUNK)], buf)
            pltpu.sync_copy(buf, out_hbm.at[d0_v.at[j]])
            pltpu.sync_copy(buf, out_hbm.at[d1_v.at[j]])

    shp = (SC_WORKERS, n_sub, SC_CHUNK)
    return k(hp, dest0.reshape(shp), dest1.reshape(shp))


def _sc_gather(table, idx):
    M = idx.shape[0]
    W = table.shape[1]
    per_w, n_sub = _sc_rows(M)
    assert n_sub % 2 == 0
    mesh = plsc.VectorSubcoreMesh(core_axis_name="c", subcore_axis_name="s")

    @functools.partial(
        pl.kernel, mesh=mesh,
        out_type=jax.ShapeDtypeStruct((M, W), table.dtype),
        scratch_types=[pltpu.VMEM((n_sub, SC_CHUNK), jnp.int32),
                       pltpu.VMEM((SC_CHUNK, W), table.dtype), pltpu.VMEM((SC_CHUNK, W), table.dtype),
                       pltpu.SemaphoreType.DMA, pltpu.SemaphoreType.DMA],
    )
    def k(table_hbm, idx_hbm, out_hbm, idx_v, buf0, buf1, sem0, sem1):
        wid = lax.axis_index("s") * SC_CORES + lax.axis_index("c")
        pltpu.sync_copy(idx_hbm.at[wid], idx_v)
        bufs, sems = (buf0, buf1), (sem0, sem1)

        def gather(j, slot):
            return pltpu.make_async_copy(table_hbm.at[idx_v.at[j]], bufs[slot], sems[slot])

        gather(0, 0).start()

        @pl.loop(0, n_sub // 2)
        def _(jj):
            for slot in range(2):
                j = jj * 2 + slot

                @pl.when(j + 1 < n_sub)
                def _():
                    gather(j + 1, 1 - slot).start()

                gather(j, slot).wait()
                pltpu.sync_copy(bufs[slot], out_hbm.at[pl.ds(wid * per_w + j * SC_CHUNK, SC_CHUNK)])

    return k(table, idx.reshape(SC_WORKERS, n_sub, SC_CHUNK))


def _moe(x1, hp, ri, rg, cnt, layer, wg, wu, wd, gfinal, final):
    T = x1.shape[0]
    bm = MOE_ROWS
    experts = jnp.arange(N_EXPERTS, dtype=jnp.int32)
    counts = cnt[0, :N_EXPERTS]
    padded = (counts + bm - 1) // bm * bm
    pad_ends = jnp.cumsum(padded)
    pad_starts = pad_ends - padded
    dest2 = ri[:, 2:4] + jnp.sum(jnp.where(ri[:, 0:2, None] > experts, padded, 0), axis=-1)
    n_blocks = (2 * T) // bm + N_EXPERTS
    block_start = jnp.arange(n_blocks, dtype=jnp.int32) * bm
    block_expert = jnp.minimum(jnp.sum((pad_ends[None, :] <= block_start[:, None]).astype(jnp.int32), axis=1),
                               N_EXPERTS - 1)
    block_rows = jnp.clip(counts[block_expert] - (block_start - pad_starts[block_expert]), 0, bm)
    xs = _sc_dispatch(hp, dest2[:, 0], dest2[:, 1], n_blocks * bm)
    y_rows = _expert_blocks(xs, block_expert, block_rows, layer, wg, wu, wd)
    yg = _sc_gather(y_rows, jnp.concatenate([dest2[:, 0], dest2[:, 1]]))
    return _moe_combine(x1, yg, rg, gfinal, final)


def _ssd_pre_kernel(x_ref, g_ref, w_ref, z_ref, xbc_ref, dt_ref):
    xn = _rms(x_ref[...], g_ref[...]).astype(BF16)
    step = 512
    for c in range(0, D_INNER, step):
        z_ref[:, c:c + step] = _dot(xn, w_ref[:, c:c + step]).astype(BF16)
    for c in range(0, XBC_DIM, step):
        xbc_ref[:, c:c + step] = _dot(xn, w_ref[:, D_INNER + c:D_INNER + c + step]).astype(BF16)
    dt_ref[...] = _dot(xn, w_ref[:, D_INNER + XBC_DIM:])


def _ssd_pre(x2, g, w_cat):
    T = x2.shape[0]
    tm = min(ROW_TILE, T)
    row = lambda n: pl.BlockSpec((tm, n), lambda i: (i, 0))
    return pl.pallas_call(
        _ssd_pre_kernel,
        grid=(T // tm,),
        in_specs=[row(D_MODEL), _full((1, D_MODEL)), _full(w_cat.shape)],
        out_specs=[row(D_INNER), row(XBC_DIM), row(LANES)],
        out_shape=[jax.ShapeDtypeStruct((T, D_INNER), BF16), jax.ShapeDtypeStruct((T, XBC_DIM), BF16),
                   jax.ShapeDtypeStruct((T, LANES), F32)],
        compiler_params=_params("parallel"),
        name="ssd_pre",
    )(x2, g, w_cat)


def _ssd_scan_kernel(xbc_ref, dt_ref, cw_ref, cb_ref, dtb_ref, alog_ref, dsk_ref, y_ref,
                     ubuf, abuf, state):
    Q = CHUNK
    H = Q // 2
    c = pl.program_id(1)
    n_blk = XBC_DIM // LANES

    @pl.when(c == 0)
    def _():
        ubuf[:, 0:8, :] = jnp.zeros((n_blk, 8, LANES), F32)
        state[...] = jnp.zeros(state.shape, F32)

    phases = 8
    for blk in range(n_blk):
        lanes = slice(blk * LANES, (blk + 1) * LANES)
        ubuf[blk, 8:8 + Q, :] = xbc_ref[0, :, lanes].astype(F32)
        taps = [cw_ref[k:k + 1, lanes] for k in range(D_CONV)]
        bias = cb_ref[:, lanes]
        for s in range(phases):
            acc = bias
            for k in range(D_CONV):
                acc = acc + taps[k] * ubuf[blk, pl.ds(8 - (D_CONV - 1) + k + s, Q // phases, stride=phases), :]
            abuf[blk, pl.ds(s, Q // phases, stride=phases), :] = acc * _sigmoid(acc)
        ubuf[blk, 0:8, :] = ubuf[blk, Q:Q + 8, :]

    x_dt = dt_ref[0] + dtb_ref[...]
    dt = jnp.maximum(x_dt, 0.0) + jnp.log(1.0 + jnp.exp(-jnp.abs(x_dt)))
    da = dt * (-jnp.exp(alog_ref[...]) * LOG2E)
    ri = lax.broadcasted_iota(jnp.int32, (Q, Q), 0)
    ci = lax.broadcasted_iota(jnp.int32, (Q, Q), 1)
    tril = jnp.where(ci <= ri, 1.0, 0.0).astype(BF16)
    d_hi = da.astype(BF16)
    r1 = da - d_hi.astype(F32)
    d_mid = r1.astype(BF16)
    d_lo = (r1 - d_mid.astype(F32)).astype(BF16)
    cs = _dot(tril, d_hi) + _dot(tril, d_mid) + _dot(tril, d_lo)
    cs_last = cs[Q - 1:Q, :]
    cs_t = jnp.transpose(cs)
    dt_t = jnp.transpose(dt)
    lw = jnp.log(dt) * LOG2E + (cs_last - cs)
    dec_last = jnp.exp2(cs_last)
    low = lax.broadcasted_iota(jnp.int32, (Q, LANES), 1) < SSD_HEADDIM
    low1 = low[0:1, :]
    tri = lax.broadcasted_iota(jnp.int32, (H, H), 1) <= lax.broadcasted_iota(jnp.int32, (H, H), 0)
    hg = SSD_HEADS // SSD_GROUPS
    x_blocks = D_INNER // LANES
    for g in range(SSD_GROUPS):
        b_g = abuf[x_blocks + g]
        c_g = abuf[x_blocks + SSD_GROUPS + g].astype(BF16)
        cb = _dot_nt(c_g, b_g.astype(BF16))
        b_t = jnp.transpose(b_g).astype(BF16)
        st = state[g]
        y_inter = _dot(c_g, st.astype(BF16))
        xw_parts, dl_parts = [], []
        for p in range(hg // 2):
            j = g * (hg // 2) + p
            lanes = slice(j * LANES, (j + 1) * LANES)
            x_p = abuf[j]
            x16 = x_p.astype(BF16)
            ys, ecs, wend = [], [], []
            for e in range(2):
                h = 2 * j + e
                col = jnp.broadcast_to(cs[:, h:h + 1], (Q, LANES))
                row = cs_t[h:h + 1, :]
                dtr = dt_t[h:h + 1, :]
                tl = jnp.exp2(jnp.where(tri, col[:H] - row[:, :H], NEG_INF)) * (cb[:H, :H] * dtr[:, :H])
                bl = jnp.exp2(col[H:] - row[:, :H]) * (cb[H:, :H] * dtr[:, :H])
                br = jnp.exp2(jnp.where(tri, col[H:] - row[:, H:], NEG_INF)) * (cb[H:, H:] * dtr[:, H:])
                y_top = _dot(tl.astype(BF16), x16[:H])
                y_bot = _dot(jnp.concatenate([bl, br], axis=1).astype(BF16), x16)
                ys.append(jnp.concatenate([y_top, y_bot], axis=0))
                ecs.append(jnp.exp2(col))
                wend.append(jnp.exp2(jnp.broadcast_to(lw[:, h:h + 1], (Q, LANES))))
            y_p = (jnp.where(low, ys[0], ys[1])
                   + y_inter[:, p * LANES:(p + 1) * LANES] * jnp.where(low, ecs[0], ecs[1])
                   + dsk_ref[:, lanes] * x_p)
            y_ref[0, :, lanes] = y_p.astype(y_ref.dtype)
            xw_parts.append((x_p * jnp.where(low, wend[0], wend[1])).astype(BF16))
            dl_parts.append(jnp.where(low1, dec_last[:, 2 * j:2 * j + 1], dec_last[:, 2 * j + 1:2 * j + 2]))
        xw = jnp.concatenate(xw_parts, axis=1)
        dl = jnp.concatenate(dl_parts, axis=1)
        state[g] = st * dl + _dot(b_t, xw)


def _ssd_scan(xbc, dt, conv_w, conv_b, dt_bias, a_log, d_skip):
    B, S, _ = xbc.shape
    Q = CHUNK
    hg = SSD_HEADS // SSD_GROUPS
    return pl.pallas_call(
        _ssd_scan_kernel,
        grid=(B, S // Q),
        in_specs=[pl.BlockSpec((1, Q, XBC_DIM), lambda b, c: (b, c, 0)),
                  pl.BlockSpec((1, Q, LANES), lambda b, c: (b, c, 0)),
                  _full((D_CONV, XBC_DIM)), _full((1, XBC_DIM)), _full((1, LANES)), _full((1, LANES)),
                  _full((1, D_INNER))],
        out_specs=pl.BlockSpec((1, Q, D_INNER), lambda b, c: (b, c, 0)),
        out_shape=jax.ShapeDtypeStruct((B, S, D_INNER), BF16),
        scratch_shapes=[pltpu.VMEM((XBC_DIM // LANES, Q + 8, LANES), F32),
                        pltpu.VMEM((XBC_DIM // LANES, Q, LANES), F32),
                        pltpu.VMEM((SSD_GROUPS, D_STATE, hg * SSD_HEADDIM), F32)],
        compiler_params=_params("parallel", "arbitrary"),
        name="ssd_scan",
    )(xbc, dt, conv_w, conv_b, dt_bias, a_log, d_skip)


def _ssd_post_kernel(y_ref, z_ref, gn_ref, x_ref, wo_ref, *tail_refs):
    gsz = D_INNER // SSD_GROUPS
    out = None
    for g in range(SSD_GROUPS):
        sl = slice(g * gsz, (g + 1) * gsz)
        z = z_ref[:, sl].astype(F32)
        yz = y_ref[:, sl].astype(F32) * (z * _sigmoid(z))
        part = _dot(_rms(yz, gn_ref[:, sl]).astype(BF16), wo_ref[sl, :])
        out = part if out is None else out + part
    _route_tail(x_ref[...] + out, *tail_refs)


def _ssd_post(y, z, gn, x2, wo, tail):
    T = x2.shape[0]
    tm = min(ROW_TILE, T)
    row = lambda n: pl.BlockSpec((tm, n), lambda i: (i, 0))
    t_in, t_out = _tail_specs(tm)
    return pl.pallas_call(
        _ssd_post_kernel,
        grid=(T // tm,),
        in_specs=[row(D_INNER), row(D_INNER), _full((1, D_INNER)), row(D_MODEL), _full(wo.shape)] + t_in,
        out_specs=t_out,
        out_shape=_tail_shapes(T),
        scratch_shapes=[pltpu.VMEM((1, LANES), F32)],
        compiler_params=_params("arbitrary"),
        name="ssd_post",
    )(y, z, gn, x2, wo, *tail)


def _rope_tables(positions):
    pos = positions.reshape(-1, 1).astype(F32)
    lane = jnp.arange(LANES)

    def tables(rot, lead, period):
        half = rot // 2
        off = lane % period - lead
        rotary = (off >= 0) & (off < rot)
        inv = ROPE_THETA ** (-(2 * (off % half)).astype(F32) / rot)
        ang = pos * jnp.where(rotary, inv, 0.0)[None, :]
        sign = jnp.where(rotary, jnp.where(off < half, -1.0, 1.0), 0.0)
        return [jnp.cos(ang), jnp.sin(ang) * sign[None, :]]

    return tables(ROPE_DIM_A, 0, HEAD_DIM_A) + tables(MLA_ROPE, MLA_NOPE, LANES)


def _pad_cols(w, n):
    return jnp.pad(w, ((0, 0), (0, n - w.shape[1])))


def _head_pad(w, width):
    K = w.shape[0]
    w = w.reshape(K, -1, width)
    return jnp.pad(w, ((0, 0), (0, 0), (0, LANES - width))).reshape(K, -1)


def _router_weights(w_group, b_group, w_router, b_router):
    w = _pad_cols(jnp.concatenate([w_router, w_group], axis=1), LANES)
    b = _pad_cols(jnp.concatenate([b_router, b_group])[None, :], LANES)
    hi = w.astype(BF16)
    lo = (w - hi.astype(F32)).astype(BF16)
    return jnp.concatenate([hi, lo], axis=1), b


def kernel(x, positions, attn_norm, w_in_attn, mla_q_norm, w_uq, mla_kv_norm, w_ukv, w_out_attn,
           ssd_norm, w_in_ssd, conv_w, conv_b, dt_bias, a_log, d_skip, gate_norm, w_out_ssd,
           moe_norm, w_group, b_group, w_router, b_router, w_gate, w_up, w_down, final_norm):
    B, S, D = x.shape
    T = B * S
    x2 = x.reshape(T, D)
    tabs = _rope_tables(positions)

    def tail_params(layer):
        wr, b = _router_weights(w_group[layer], b_group[layer], w_router[layer], b_router[layer])
        return [moe_norm[layer][None, :], wr, b]

    w_in = w_in_attn[0]
    kr_off = 3 * MIX_A + MLA_Q_RANK + MLA_KV_RANK
    kr_cols = jnp.pad(w_in[:, kr_off:], ((0, 0), (MLA_NOPE, LANES - MLA_NOPE - MLA_ROPE)))
    w_cat = jnp.concatenate([w_in[:, :kr_off], kr_cols], axis=1).astype(BF16)
    wuq = _head_pad(w_uq[0], MLA_NOPE + MLA_ROPE).astype(BF16)
    ukv = w_ukv[0].reshape(MLA_KV_RANK, MLA_HEADS, MLA_NOPE + MLA_V)
    wuk = _head_pad(ukv[:, :, :MLA_NOPE].reshape(MLA_KV_RANK, -1), MLA_NOPE).astype(BF16)
    wuv = _head_pad(ukv[:, :, MLA_NOPE:].reshape(MLA_KV_RANK, -1), MLA_V).astype(BF16)
    pre = _attn_pre(x2, attn_norm[0][None, :], w_cat, mla_q_norm[0][None, :], wuq,
                    mla_kv_norm[0][None, :], wuk, wuv, tabs, S)
    qm, km, vm = pre[9:]
    shp = lambda t: t.reshape(B, S, t.shape[-1])
    qkv = [[t.reshape(B, 1, S, MIX_A) for t in pre[0:3]], pre[3:6], pre[6:9]]
    prev = None
    for n, d in enumerate(DILATIONS):
        prev = _dilated_branch(*qkv[n], d, prev, n == len(DILATIONS) - 1)
    ob = _mla_attention(shp(qm), shp(km), shp(vm)).reshape(T, MIX_B)
    x1, h, ri, rg, cnt = _attn_post(prev[0], ob, x2, w_out_attn[0].astype(BF16), tail_params(0))
    x2 = _moe(x1, h, ri, rg, cnt, 0, w_gate, w_up, w_down, final_norm[None, :], False)

    w_cat = _pad_cols(w_in_ssd[0], D_INNER + XBC_DIM + LANES).astype(BF16)
    z, xbc, dt = _ssd_pre(x2, ssd_norm[0][None, :], w_cat)
    y = _ssd_scan(xbc.reshape(B, S, XBC_DIM), dt.reshape(B, S, LANES), conv_w[0], conv_b[0][None, :],
                  _pad_cols(dt_bias[0][None, :], LANES), _pad_cols(a_log[0][None, :], LANES),
                  jnp.repeat(d_skip[0], SSD_HEADDIM)[None, :])
    x1, h, ri, rg, cnt = _ssd_post(y.reshape(T, D_INNER), z, gate_norm[0][None, :], x2,
                                   w_out_ssd[0].astype(BF16), tail_params(1))
    out = _moe(x1, h, ri, rg, cnt, 1, w_gate, w_up, w_down, final_norm[None, :], True)
    return out.reshape(B, S, D)
```

```python
import functools

import jax
import jax.numpy as jnp
from jax import lax
from jax.experimental import pallas as pl
from jax.experimental.pallas import tpu as pltpu
from jax.experimental.pallas import tpu_sc as plsc

F32 = jnp.float32
BF16 = jnp.bfloat16

D_MODEL = 1024
HEADS_A = 8
HEAD_DIM_A = 64
ROPE_DIM_A = 16
DILATIONS = (1, 4, 16)
SPAN = 128
DIL_ROWS = 2048
MLA_HEADS = 8
MLA_NOPE = 64
MLA_ROPE = 32
MLA_V = 64
MLA_Q_RANK = 256
MLA_KV_RANK = 128
ROPE_THETA = 500000.0
MIX_A = HEADS_A * HEAD_DIM_A
MIX_B = MLA_HEADS * MLA_V
D_INNER = 2048
SSD_HEADDIM = 64
SSD_HEADS = 32
SSD_GROUPS = 4
D_STATE = 128
D_CONV = 4
CHUNK = 256
XBC_DIM = D_INNER + 2 * SSD_GROUPS * D_STATE
N_GROUPS = 4
EXPERTS_PER_GROUP = 8
N_EXPERTS = 32
D_EXPERT = 512
NORM_EPS = 1e-6

LANES = 128
VMEM_LIMIT = 56 * 1024 * 1024
ROW_TILE = 512
MOE_ROWS = 512
SC_CORES = 2
SC_WORKERS = 32
SC_CHUNK = 64
MLA_TQ = 2048
MLA_TK = 512
MLA_UNROLL = 2
NEG_INF = float("-inf")
LOG2E = 1.4426950408889634


def _params(*sem):
    return pltpu.CompilerParams(dimension_semantics=sem, vmem_limit_bytes=VMEM_LIMIT)


def _dot(a, b):
    return jnp.dot(a, b, preferred_element_type=F32)


def _dot_nt(a, b):
    return lax.dot_general(a, b, (((1,), (1,)), ((), ())), preferred_element_type=F32)


def _rms(x, g):
    return x * lax.rsqrt(jnp.mean(x * x, axis=-1, keepdims=True) + NORM_EPS) * g


def _sigmoid(x):
    return 1.0 / (1.0 + jnp.exp(-x))


def _rope(t, c, s, half, first):
    partner = jnp.where(first, pltpu.roll(t, LANES - half, 1), pltpu.roll(t, half, 1))
    return t * c + partner * s


def _pack_bf16_pairs(x):
    n = x.shape[1] // 2
    bits = lax.bitcast_convert_type(x.astype(BF16).astype(F32), jnp.uint32)
    return (bits[:, :n] >> 16) | (bits[:, n:] & jnp.uint32(0xFFFF0000))


def _unpack_bf16_pairs(w):
    lo = lax.bitcast_convert_type(w << 16, F32)
    hi = lax.bitcast_convert_type(w & jnp.uint32(0xFFFF0000), F32)
    return jnp.concatenate([lo, hi], axis=1)


def _full(shape):
    return pl.BlockSpec(shape, lambda *_: (0,) * len(shape))


def _attn_pre_kernel(x_ref, g_ref, w_ref, qn_ref, wuq_ref, kvn_ref, wuk_ref, wuv_ref,
                     ca_ref, sa_ref, cb_ref, sb_ref,
                     qa_ref, ka_ref, va_ref, q4_ref, k4_ref, v4_ref, q16_ref, k16_ref, v16_ref,
                     qm_ref, km_ref, vm_ref, lay):
    xn = _rms(x_ref[...], g_ref[...]).astype(BF16)
    ca, sa, cb, sb = ca_ref[...], sa_ref[...], cb_ref[...], sb_ref[...]
    half_a = ROPE_DIM_A // 2
    half_b = MLA_ROPE // 2
    lane = lax.broadcasted_iota(jnp.int32, ca.shape, 1)
    first_a = lane % HEAD_DIM_A < half_a
    first_b = (lane >= MLA_NOPE) & (lane < MLA_NOPE + half_b)

    q = _dot(xn, w_ref[:, 0:MIX_A]) * (HEAD_DIM_A ** -0.5)
    k = _dot(xn, w_ref[:, MIX_A:2 * MIX_A])
    v = _dot(xn, w_ref[:, 2 * MIX_A:3 * MIX_A])
    for c in range(MIX_A // LANES):
        sl = slice(c * LANES, (c + 1) * LANES)
        lay[0, c] = _rope(q[:, sl], ca, sa, half_a, first_a)
        lay[1, c] = _rope(k[:, sl], ca, sa, half_a, first_a)
        lay[2, c] = v[:, sl]
    tm = x_ref.shape[0]
    for a, (nat, by4, by16) in enumerate(((qa_ref, q4_ref, q16_ref), (ka_ref, k4_ref, k16_ref),
                                          (va_ref, v4_ref, v16_ref))):
        for c in range(MIX_A // LANES):
            sl = slice(c * LANES, (c + 1) * LANES)
            nat[:, sl] = lay[a, c].astype(BF16)
            for d, ref in ((4, by4), (16, by16)):
                for r in range(d):
                    ref[0, r, :, sl] = lay[a, c, pl.ds(r, tm // d, stride=d), :].astype(BF16)

    o = 3 * MIX_A
    cq = _dot(xn, w_ref[:, o:o + MLA_Q_RANK])
    ckv = _dot(xn, w_ref[:, o + MLA_Q_RANK:o + MLA_Q_RANK + MLA_KV_RANK])
    kr = _dot(xn, w_ref[:, o + MLA_Q_RANK + MLA_KV_RANK:o + MLA_Q_RANK + MLA_KV_RANK + LANES])
    cqn = _rms(cq, qn_ref[...]).astype(BF16)
    kvn = _rms(ckv, kvn_ref[...]).astype(BF16)
    krr = _rope(kr, cb, sb, half_b, first_b)
    qm = _dot(cqn, wuq_ref[...]) * ((MLA_NOPE + MLA_ROPE) ** -0.5 * LOG2E)
    km = _dot(kvn, wuk_ref[...])
    for h in range(MLA_HEADS):
        sl = slice(h * LANES, (h + 1) * LANES)
        qm_ref[:, sl] = _rope(qm[:, sl], cb, sb, half_b, first_b).astype(BF16)
        km_ref[:, sl] = (km[:, sl] + krr).astype(BF16)
    vm = _dot(kvn, wuv_ref[...])
    low = lax.broadcasted_iota(jnp.int32, vm.shape, 1) % LANES < MLA_V
    vm_ref[...] = jnp.where(low, vm, 1.0).astype(BF16)


def _attn_pre(x2, g, w_cat, qn, wuq, kvn, wuk, wuv, tabs, seq):
    T = x2.shape[0]
    tm = min(ROW_TILE, T)
    nb = seq // tm
    row = lambda n: pl.BlockSpec((tm, n), lambda i: (i, 0))
    wide = MLA_HEADS * LANES
    by_residue = [pl.BlockSpec((1, d, tm // d, MIX_A), lambda i: (i // nb, 0, i % nb, 0))
                  for d in DILATIONS[1:]]
    residue_shapes = [jax.ShapeDtypeStruct((T // seq, d, seq // d, MIX_A), BF16) for d in DILATIONS[1:]]
    return pl.pallas_call(
        _attn_pre_kernel,
        grid=(T // tm,),
        in_specs=[row(D_MODEL), _full((1, D_MODEL)), _full(w_cat.shape), _full((1, MLA_Q_RANK)),
                  _full(wuq.shape), _full((1, MLA_KV_RANK)), _full(wuk.shape), _full(wuv.shape)]
                 + [row(LANES)] * 4,
        out_specs=[row(MIX_A)] * 3 + [by_residue[0]] * 3 + [by_residue[1]] * 3 + [row(wide)] * 3,
        out_shape=[jax.ShapeDtypeStruct((T, MIX_A), BF16)] * 3 + [residue_shapes[0]] * 3
                  + [residue_shapes[1]] * 3 + [jax.ShapeDtypeStruct((T, wide), BF16)] * 3,
        scratch_shapes=[pltpu.VMEM((3, MIX_A // LANES, tm, LANES), F32)],
        compiler_params=_params("parallel"),
        name="attn_pre",
    )(x2, g, w_cat, qn, wuq, kvn, wuk, wuv, *tabs)


def _dilated_kernel(*refs, d, tq, carry, last):
    refs = list(refs)
    q_ref, kc_ref, kp_ref, vc_ref, vp_ref = refs[:5]
    op_ref, lp_ref = refs[5:7] if carry else (None, None)
    kbuf, vbuf = refs[-2:]
    o_ref = refs[7] if carry else refs[5]
    lse_ref = None if last else refs[-3]
    j = pl.program_id(2)
    for r in range(d):
        kbuf[r, 0:SPAN, :] = kp_ref[0, r]
        kbuf[r, SPAN:, :] = kc_ref[0, r]
        vbuf[r, 0:SPAN, :] = vp_ref[0, r]
        vbuf[r, SPAN:, :] = vc_ref[0, r]
    low = lax.broadcasted_iota(jnp.int32, (SPAN, LANES), 1) < HEAD_DIM_A
    qi = lax.broadcasted_iota(jnp.int32, (SPAN, 2 * SPAN), 0)
    ki = lax.broadcasted_iota(jnp.int32, (SPAN, 2 * SPAN), 1)
    band = (ki >= qi) & (ki <= qi + SPAN)
    first_valid = band & ((ki >= SPAN) | (j > 0))
    for r in range(d):
        for sub in range(tq // SPAN):
            valid = first_valid if sub == 0 else band
            q2 = q_ref[0, r, sub * SPAN:(sub + 1) * SPAN, :]
            kk = kbuf[r, sub * SPAN:(sub + 2) * SPAN, :]
            vv = vbuf[r, sub * SPAN:(sub + 2) * SPAN, :]
            os_, ls_ = [], []
            for e in range(2):
                qe = jnp.where(low if e == 0 else jnp.logical_not(low), q2, jnp.zeros_like(q2))
                s = jnp.where(valid, _dot_nt(qe, kk), NEG_INF)
                m = jnp.max(s, axis=-1, keepdims=True)
                pr = jnp.exp(s - m)
                l = jnp.sum(pr, axis=-1, keepdims=True)
                os_.append(_dot(pr.astype(BF16), vv) / l)
                ls_.append(m + jnp.log(l))
            o2 = jnp.where(low, os_[0], os_[1])
            l2 = jnp.where(low, ls_[0], ls_[1])
            rows = pl.ds(sub * SPAN, SPAN) if d == 1 else pl.ds(d * sub * SPAN + r, SPAN, stride=d)
            if carry:
                lp = lp_ref[0, 0, rows, :]
                mx = jnp.maximum(lp, l2)
                wp = jnp.exp(lp - mx)
                wc = jnp.exp(l2 - mx)
                den = wp + wc
                o2 = (wp * op_ref[0, 0, rows, :] + wc * o2) / den
                l2 = mx + jnp.log(den)
            o_ref[0, 0, rows, :] = o2
            if not last:
                lse_ref[0, 0, rows, :] = l2


def _dilated_branch(q, k, v, d, prev, last):
    B, _, L, _ = q.shape
    pairs = MIX_A // LANES
    tq = DIL_ROWS // d
    cur = pl.BlockSpec((1, d, tq, LANES), lambda b, p, j: (b, 0, j, p))
    prv = pl.BlockSpec((1, d, SPAN, LANES), lambda b, p, j: (b, 0, jnp.maximum(j * (tq // SPAN) - 1, 0), p))
    nat = pl.BlockSpec((1, 1, DIL_ROWS, LANES), lambda b, p, j: (b, p, j, 0))
    nat_shape = jax.ShapeDtypeStruct((B, pairs, d * L, LANES), F32)
    carry = prev is not None
    outs = pl.pallas_call(
        functools.partial(_dilated_kernel, d=d, tq=tq, carry=carry, last=last),
        grid=(B, pairs, L // tq),
        in_specs=[cur, cur, prv, cur, prv] + ([nat, nat] if carry else []),
        out_specs=[nat] if last else [nat, nat],
        out_shape=[nat_shape] if last else [nat_shape, nat_shape],
        scratch_shapes=[pltpu.VMEM((d, tq + SPAN, LANES), BF16), pltpu.VMEM((d, tq + SPAN, LANES), BF16)],
        compiler_params=_params("parallel", "parallel", "parallel"),
        name=f"dilated_d{d}",
    )(q, k, k, v, v, *(prev if carry else ()))
    return outs


def _mla_kernel(q_ref, k_ref, v_ref, o_ref, m0, m1, a0, a1, *, tq, tk):
    qi = pl.program_id(2)
    ms, accs = (m0, m1), (a0, a1)
    for e in range(2):
        ms[e][...] = jnp.full(ms[e].shape, NEG_INF, F32)
        accs[e][...] = jnp.zeros(accs[e].shape, F32)

    def step(kv, diag):
        start = pl.multiple_of(kv * tk, tk)
        r0 = 0 if diag is None else diag * tk
        for e in range(2):
            lanes = slice(e * LANES, (e + 1) * LANES)
            s = _dot_nt(q_ref[0, r0:, lanes], k_ref[0, pl.ds(start, tk), lanes])
            if diag is not None:
                row = lax.broadcasted_iota(jnp.int32, s.shape, 0)
                col = lax.broadcasted_iota(jnp.int32, s.shape, 1)
                s = jnp.where(col <= row, s, NEG_INF)
            m_prev = ms[e][r0:, :]
            m_new = jnp.maximum(m_prev, jnp.max(s, axis=-1, keepdims=True))
            alpha = jnp.exp2(m_prev - m_new)
            pr = jnp.concatenate([jnp.exp2(s[:, c * LANES:(c + 1) * LANES] - m_new)
                                  for c in range(tk // LANES)], axis=1).astype(BF16)
            accs[e][r0:, :] = alpha * accs[e][r0:, :] + _dot(pr, v_ref[0, pl.ds(start, tk), lanes])
            ms[e][r0:, :] = m_new

    n_full = qi * (tq // tk)
    n_main = n_full // MLA_UNROLL

    def body(i, c):
        for u in range(MLA_UNROLL):
            step(MLA_UNROLL * i + u, None)
        return c

    def single(i, c):
        step(i, None)
        return c

    lax.fori_loop(0, n_main, body, 0)
    lax.fori_loop(n_main * MLA_UNROLL, n_full, single, 0)

    for t in range(tq // tk):
        step(n_full + t, t)
    low = lax.broadcasted_iota(jnp.int32, (tq, LANES), 1) < MLA_V
    r0 = a0[...] / pltpu.roll(a0[...], MLA_V, 1)
    r1 = a1[...] / pltpu.roll(a1[...], MLA_V, 1)
    o_ref[0] = jnp.where(low, r0, pltpu.roll(r1, MLA_V, 1)).astype(BF16)


def _mla_attention(qm, km, vm):
    B, S, _ = qm.shape
    tq = min(MLA_TQ, S)
    tk = min(MLA_TK, tq)
    pairs = MLA_HEADS // 2
    return pl.pallas_call(
        functools.partial(_mla_kernel, tq=tq, tk=tk),
        grid=(B, pairs, S // tq),
        in_specs=[pl.BlockSpec((1, tq, 2 * LANES), lambda b, p, i: (b, i, p)),
                  pl.BlockSpec((1, S, 2 * LANES), lambda b, p, i: (b, 0, p)),
                  pl.BlockSpec((1, S, 2 * LANES), lambda b, p, i: (b, 0, p))],
        out_specs=pl.BlockSpec((1, tq, LANES), lambda b, p, i: (b, i, p)),
        out_shape=jax.ShapeDtypeStruct((B, S, MIX_B), BF16),
        scratch_shapes=[pltpu.VMEM((tq, LANES), F32)] * 4,
        compiler_params=_params("parallel", "parallel", "arbitrary"),
        name="mla_flash",
    )(qm, km, vm)


def _route_tail(x1, gm_ref, wr_ref, br_ref, x1_ref, h_ref, ri_ref, rg_ref, cnt_ref, cnt_sc):
    x1_ref[...] = x1
    hn = _rms(x1, gm_ref[...])
    h_hi = hn.astype(BF16)
    h_ref[...] = _pack_bf16_pairs(hn)
    h_lo = (hn - h_hi.astype(F32)).astype(BF16)
    both = _dot(h_hi, wr_ref[...])
    logits = both[:, :LANES] + both[:, LANES:] + _dot(h_lo, wr_ref[:, 0:LANES]) + br_ref[...]
    tm = logits.shape[0]
    lane = lax.broadcasted_iota(jnp.int32, (tm, LANES), 1)
    lanef = lane.astype(F32)
    big = float(LANES)
    is_g = (lane >= N_EXPERTS) & (lane < N_EXPERTS + N_GROUPS)
    gl = jnp.where(is_g, logits, NEG_INF)
    gmax = jnp.max(gl, axis=-1, keepdims=True)
    gsum = jnp.sum(jnp.exp(gl - gmax), axis=-1, keepdims=True)
    g_val = 1.0 / gsum
    g_idx = jnp.min(jnp.where(gl == gmax, lanef, big), axis=-1, keepdims=True) - float(N_EXPERTS)
    in_grp = (lane < N_EXPERTS) & ((lane // EXPERTS_PER_GROUP).astype(F32) == g_idx)
    el = jnp.where(in_grp, logits, NEG_INF)
    e1 = jnp.max(el, axis=-1, keepdims=True)
    i1 = jnp.min(jnp.where(el == e1, lanef, big), axis=-1, keepdims=True)
    el2 = jnp.where(lanef == i1, NEG_INF, el)
    e2 = jnp.max(el2, axis=-1, keepdims=True)
    i2 = jnp.min(jnp.where(el2 == e2, lanef, big), axis=-1, keepdims=True)
    t = jnp.exp(e2 - e1)
    p1 = 1.0 / (1.0 + t)
    rg_ref[...] = jnp.where(lane == 0, p1 * g_val, jnp.where(lane == 1, t * p1 * g_val, 0.0))
    @pl.when(pl.program_id(0) == 0)
    def _():
        cnt_sc[...] = jnp.zeros(cnt_sc.shape, F32)

    oh1 = lanef == i1
    oh2 = lanef == i2
    oh = jnp.where(oh1 | oh2, 1.0, 0.0)
    rr = lax.broadcasted_iota(jnp.int32, (tm, tm), 0)
    cc = lax.broadcasted_iota(jnp.int32, (tm, tm), 1)
    strict = jnp.where(cc < rr, 1.0, 0.0).astype(BF16)
    before = _dot(strict, oh.astype(BF16)) + cnt_sc[...]
    rank1 = jnp.sum(jnp.where(oh1, before, 0.0), axis=-1, keepdims=True)
    rank2 = jnp.sum(jnp.where(oh2, before, 0.0), axis=-1, keepdims=True)
    cnt_sc[...] = cnt_sc[...] + jnp.sum(oh, axis=0, keepdims=True)
    cnt_ref[...] = jnp.broadcast_to(cnt_sc[...], cnt_ref.shape).astype(jnp.int32)
    ri_ref[...] = jnp.where(lane == 0, i1, jnp.where(lane == 1, i2, jnp.where(
        lane == 2, rank1, jnp.where(lane == 3, rank2, 0.0)))).astype(jnp.int32)


def _tail_specs(tm):
    row = lambda n: pl.BlockSpec((tm, n), lambda i: (i, 0))
    in_specs = [_full((1, D_MODEL)), _full((D_MODEL, 2 * LANES)), _full((1, LANES))]
    out_specs = [row(D_MODEL), row(D_MODEL // 2), row(LANES), row(LANES), _full((8, LANES))]
    return in_specs, out_specs


def _tail_shapes(T):
    return [jax.ShapeDtypeStruct((T, D_MODEL), F32), jax.ShapeDtypeStruct((T, D_MODEL // 2), jnp.uint32),
            jax.ShapeDtypeStruct((T, LANES), jnp.int32), jax.ShapeDtypeStruct((T, LANES), F32),
            jax.ShapeDtypeStruct((8, LANES), jnp.int32)]


def _attn_post_kernel(oa_ref, ob_ref, x_ref, wo_ref, *tail_refs):
    oa = jnp.concatenate([oa_ref[0, p] for p in range(MIX_A // LANES)], axis=1).astype(BF16)
    y = _dot(oa, wo_ref[0:MIX_A, :]) + _dot(ob_ref[...], wo_ref[MIX_A:, :])
    _route_tail(x_ref[...] + y, *tail_refs)


def _attn_post(oa, ob, x2, wo, tail):
    T = x2.shape[0]
    tm = min(ROW_TILE, T)
    nb = oa.shape[2] // tm
    row = lambda n: pl.BlockSpec((tm, n), lambda i: (i, 0))
    t_in, t_out = _tail_specs(tm)
    return pl.pallas_call(
        _attn_post_kernel,
        grid=(T // tm,),
        in_specs=[pl.BlockSpec((1, MIX_A // LANES, tm, LANES), lambda i: (i // nb, 0, i % nb, 0)),
                  row(MIX_B), row(D_MODEL), _full(wo.shape)] + t_in,
        out_specs=t_out,
        out_shape=_tail_shapes(T),
        scratch_shapes=[pltpu.VMEM((1, LANES), F32)],
        compiler_params=_params("arbitrary"),
        name="attn_post",
    )(oa, ob, x2, wo, *tail)


def _expert_kernel(be_ref, nr_ref, xs_ref, wg_ref, wu_ref, wd_ref, y_ref, wg_bf, wu_bf, wd_bf):
    i = pl.program_id(0)

    @pl.when((i == 0) | (be_ref[i] != be_ref[jnp.maximum(i - 1, 0)]))
    def _():
        wg_bf[...] = wg_ref[0].astype(BF16)
        wu_bf[...] = wu_ref[0].astype(BF16)
        wd_bf[...] = wd_ref[0].astype(BF16)

    @pl.when(nr_ref[i] > 0)
    def _():
        live = lax.broadcasted_iota(jnp.int32, xs_ref.shape, 0) < nr_ref[i]
        xb = _unpack_bf16_pairs(jnp.where(live, xs_ref[...], jnp.uint32(0))).astype(BF16)
        g = _dot(xb, wg_bf[...])
        u = _dot(xb, wu_bf[...])
        hid = (g * _sigmoid(g) * u).astype(BF16)
        y_ref[...] = _pack_bf16_pairs(_dot(hid, wd_bf[...]))

    @pl.when(nr_ref[i] == 0)
    def _():
        y_ref[...] = jnp.zeros(y_ref.shape, y_ref.dtype)


def _expert_blocks(xs, block_expert, block_rows, layer, wg, wu, wd):
    rows = xs.shape[0]
    bm = MOE_ROWS
    half = D_MODEL // 2
    grid_spec = pltpu.PrefetchScalarGridSpec(
        num_scalar_prefetch=2,
        grid=(rows // bm,),
        in_specs=[pl.BlockSpec((bm, half), lambda i, be, nr: (i, 0)),
                  pl.BlockSpec((None, 1, D_MODEL, D_EXPERT), lambda i, be, nr: (layer, be[i], 0, 0)),
                  pl.BlockSpec((None, 1, D_MODEL, D_EXPERT), lambda i, be, nr: (layer, be[i], 0, 0)),
                  pl.BlockSpec((None, 1, D_EXPERT, D_MODEL), lambda i, be, nr: (layer, be[i], 0, 0))],
        out_specs=pl.BlockSpec((bm, half), lambda i, be, nr: (i, 0)),
        scratch_shapes=[pltpu.VMEM((D_MODEL, D_EXPERT), BF16), pltpu.VMEM((D_MODEL, D_EXPERT), BF16),
                        pltpu.VMEM((D_EXPERT, D_MODEL), BF16)],
    )
    return pl.pallas_call(
        _expert_kernel,
        grid_spec=grid_spec,
        out_shape=jax.ShapeDtypeStruct((rows, half), jnp.uint32),
        compiler_params=_params("arbitrary"),
        name="moe_experts",
    )(block_expert, block_rows, xs, wg, wu, wd)


def _moe_sum(x_ref, r0_ref, r1_ref, rg_ref):
    rg = rg_ref[...]
    return (x_ref[...] + rg[:, 0:1] * _unpack_bf16_pairs(r0_ref[...])
            + rg[:, 1:2] * _unpack_bf16_pairs(r1_ref[...]))


def _combine_kernel(x_ref, r0_ref, r1_ref, rg_ref, gf_ref, o_ref):
    o_ref[...] = _rms(_moe_sum(x_ref, r0_ref, r1_ref, rg_ref), gf_ref[...])


def _moe_combine(x1, yg, rg, gfinal):
    T = x1.shape[0]
    tm = min(ROW_TILE, T)
    half = D_MODEL // 2
    row = lambda n: pl.BlockSpec((tm, n), lambda i: (i, 0))
    return pl.pallas_call(
        _combine_kernel,
        grid=(T // tm,),
        in_specs=[row(D_MODEL), row(half), pl.BlockSpec((tm, half), lambda i: (i + T // tm, 0)),
                  row(LANES), _full((1, D_MODEL))],
        out_specs=row(D_MODEL),
        out_shape=jax.ShapeDtypeStruct((T, D_MODEL), F32),
        compiler_params=_params("parallel"),
        name="moe_combine",
    )(x1, yg, yg, rg, gfinal)


def _sc_rows(n_rows):
    per_worker = n_rows // SC_WORKERS
    assert n_rows % SC_WORKERS == 0 and per_worker % SC_CHUNK == 0
    return per_worker, per_worker // SC_CHUNK


def _sc_dispatch(hp, dest0, dest1, rows):
    T, W = hp.shape
    per_w, n_sub = _sc_rows(T)
    mesh = plsc.VectorSubcoreMesh(core_axis_name="c", subcore_axis_name="s")

    @functools.partial(
        pl.kernel, mesh=mesh,
        out_type=jax.ShapeDtypeStruct((rows, W), hp.dtype),
        scratch_types=[pltpu.VMEM((n_sub, SC_CHUNK), jnp.int32), pltpu.VMEM((n_sub, SC_CHUNK), jnp.int32),
                       pltpu.VMEM((SC_CHUNK, W), hp.dtype)],
    )
    def k(hp_hbm, d0_hbm, d1_hbm, out_hbm, d0_v, d1_v, buf):
        wid = lax.axis_index("s") * SC_CORES + lax.axis_index("c")
        pltpu.sync_copy(d0_hbm.at[wid], d0_v)
        pltpu.sync_copy(d1_hbm.at[wid], d1_v)

        @pl.loop(0, n_sub)
        def _(j):
            pltpu.sync_copy(hp_hbm.at[pl.ds(wid * per_w + j * SC_CHUNK, SC_CHUNK)], buf)
            pltpu.sync_copy(buf, out_hbm.at[d0_v.at[j]])
            pltpu.sync_copy(buf, out_hbm.at[d1_v.at[j]])

    shp = (SC_WORKERS, n_sub, SC_CHUNK)
    return k(hp, dest0.reshape(shp), dest1.reshape(shp))


def _sc_gather(table, idx):
    M = idx.shape[0]
    W = table.shape[1]
    per_w, n_sub = _sc_rows(M)
    assert n_sub % 2 == 0
    mesh = plsc.VectorSubcoreMesh(core_axis_name="c", subcore_axis_name="s")

    @functools.partial(
        pl.kernel, mesh=mesh,
        out_type=jax.ShapeDtypeStruct((M, W), table.dtype),
        scratch_types=[pltpu.VMEM((n_sub, SC_CHUNK), jnp.int32),
                       pltpu.VMEM((SC_CHUNK, W), table.dtype), pltpu.VMEM((SC_CHUNK, W), table.dtype),
                       pltpu.SemaphoreType.DMA, pltpu.SemaphoreType.DMA],
    )
    def k(table_hbm, idx_hbm, out_hbm, idx_v, buf0, buf1, sem0, sem1):
        wid = lax.axis_index("s") * SC_CORES + lax.axis_index("c")
        pltpu.sync_copy(idx_hbm.at[wid], idx_v)
        bufs, sems = (buf0, buf1), (sem0, sem1)

        def gather(j, slot):
            return pltpu.make_async_copy(table_hbm.at[idx_v.at[j]], bufs[slot], sems[slot])

        gather(0, 0).start()

        @pl.loop(0, n_sub // 2)
        def _(jj):
            for slot in range(2):
                j = jj * 2 + slot

                @pl.when(j + 1 < n_sub)
                def _():
                    gather(j + 1, 1 - slot).start()

                gather(j, slot).wait()
                pltpu.sync_copy(bufs[slot], out_hbm.at[pl.ds(wid * per_w + j * SC_CHUNK, SC_CHUNK)])

    return k(table, idx.reshape(SC_WORKERS, n_sub, SC_CHUNK))


def _moe_rows(hp, ri, cnt, layer, wg, wu, wd):
    T = hp.shape[0]
    bm = MOE_ROWS
    experts = jnp.arange(N_EXPERTS, dtype=jnp.int32)
    counts = cnt[0, :N_EXPERTS]
    padded = (counts + bm - 1) // bm * bm
    pad_ends = jnp.cumsum(padded)
    pad_starts = pad_ends - padded
    dest2 = ri[:, 2:4] + jnp.sum(jnp.where(ri[:, 0:2, None] > experts, padded, 0), axis=-1)
    n_blocks = (2 * T) // bm + N_EXPERTS
    block_start = jnp.arange(n_blocks, dtype=jnp.int32) * bm
    block_expert = jnp.minimum(jnp.sum((pad_ends[None, :] <= block_start[:, None]).astype(jnp.int32), axis=1),
                               N_EXPERTS - 1)
    block_rows = jnp.clip(counts[block_expert] - (block_start - pad_starts[block_expert]), 0, bm)
    xs = _sc_dispatch(hp, dest2[:, 0], dest2[:, 1], n_blocks * bm)
    y_rows = _expert_blocks(xs, block_expert, block_rows, layer, wg, wu, wd)
    return _sc_gather(y_rows, jnp.concatenate([dest2[:, 0], dest2[:, 1]]))


def _ssd_pre_kernel(x_ref, r0_ref, r1_ref, rg_ref, g_ref, w_ref, x2_ref, z_ref, xbc_ref, dt_ref):
    x2 = _moe_sum(x_ref, r0_ref, r1_ref, rg_ref)
    x2_ref[...] = x2
    xn = _rms(x2, g_ref[...]).astype(BF16)
    step = 512
    for c in range(0, D_INNER, step):
        z_ref[:, c:c + step] = _dot(xn, w_ref[:, c:c + step]).astype(BF16)
    for c in range(0, XBC_DIM, step):
        xbc_ref[:, c:c + step] = _dot(xn, w_ref[:, D_INNER + c:D_INNER + c + step]).astype(BF16)
    dt_ref[...] = _dot(xn, w_ref[:, D_INNER + XBC_DIM:])


def _ssd_pre(x1, yg, rg, g, w_cat):
    T = x1.shape[0]
    tm = min(ROW_TILE, T)
    half = D_MODEL // 2
    row = lambda n: pl.BlockSpec((tm, n), lambda i: (i, 0))
    return pl.pallas_call(
        _ssd_pre_kernel,
        grid=(T // tm,),
        in_specs=[row(D_MODEL), row(half), pl.BlockSpec((tm, half), lambda i: (i + T // tm, 0)), row(LANES),
                  _full((1, D_MODEL)), _full(w_cat.shape)],
        out_specs=[row(D_MODEL), row(D_INNER), row(XBC_DIM), row(LANES)],
        out_shape=[jax.ShapeDtypeStruct((T, D_MODEL), F32), jax.ShapeDtypeStruct((T, D_INNER), BF16),
                   jax.ShapeDtypeStruct((T, XBC_DIM), BF16), jax.ShapeDtypeStruct((T, LANES), F32)],
        compiler_params=_params("parallel"),
        name="ssd_pre",
    )(x1, yg, yg, rg, g, w_cat)


def _ssd_scan_kernel(xbc_ref, dt_ref, cw_ref, cb_ref, dtb_ref, alog_ref, dsk_ref, y_ref,
                     ubuf, abuf, state):
    Q = CHUNK
    H = Q // 2
    c = pl.program_id(1)
    n_blk = XBC_DIM // LANES

    @pl.when(c == 0)
    def _():
        ubuf[:, 0:8, :] = jnp.zeros((n_blk, 8, LANES), F32)
        state[...] = jnp.zeros(state.shape, F32)

    phases = 8
    for blk in range(n_blk):
        lanes = slice(blk * LANES, (blk + 1) * LANES)
        ubuf[blk, 8:8 + Q, :] = xbc_ref[0, :, lanes].astype(F32)
        taps = [cw_ref[k:k + 1, lanes] for k in range(D_CONV)]
        bias = cb_ref[:, lanes]
        for s in range(phases):
            acc = bias
            for k in range(D_CONV):
                acc = acc + taps[k] * ubuf[blk, pl.ds(8 - (D_CONV - 1) + k + s, Q // phases, stride=phases), :]
            abuf[blk, pl.ds(s, Q // phases, stride=phases), :] = acc * _sigmoid(acc)
        ubuf[blk, 0:8, :] = ubuf[blk, Q:Q + 8, :]

    x_dt = dt_ref[0] + dtb_ref[...]
    dt = jnp.maximum(x_dt, 0.0) + jnp.log(1.0 + jnp.exp(-jnp.abs(x_dt)))
    da = dt * (-jnp.exp(alog_ref[...]) * LOG2E)
    ri = lax.broadcasted_iota(jnp.int32, (Q, Q), 0)
    ci = lax.broadcasted_iota(jnp.int32, (Q, Q), 1)
    tril = jnp.where(ci <= ri, 1.0, 0.0).astype(BF16)
    d_hi = da.astype(BF16)
    r1 = da - d_hi.astype(F32)
    d_mid = r1.astype(BF16)
    d_lo = (r1 - d_mid.astype(F32)).astype(BF16)
    cs = _dot(tril, d_hi) + _dot(tril, d_mid) + _dot(tril, d_lo)
    cs_last = cs[Q - 1:Q, :]
    cs_t = jnp.transpose(cs)
    dt_t = jnp.transpose(dt)
    lw = jnp.log(dt) * LOG2E + (cs_last - cs)
    dec_last = jnp.exp2(cs_last)
    low = lax.broadcasted_iota(jnp.int32, (Q, LANES), 1) < SSD_HEADDIM
    low1 = low[0:1, :]
    tri = lax.broadcasted_iota(jnp.int32, (H, H), 1) <= lax.broadcasted_iota(jnp.int32, (H, H), 0)
    hg = SSD_HEADS // SSD_GROUPS
    x_blocks = D_INNER // LANES
    for g in range(SSD_GROUPS):
        b_g = abuf[x_blocks + g]
        c_g = abuf[x_blocks + SSD_GROUPS + g].astype(BF16)
        cb = _dot_nt(c_g, b_g.astype(BF16))
        b_t = jnp.transpose(b_g).astype(BF16)
        st = state[g]
        y_inter = _dot(c_g, st.astype(BF16))
        xw_parts, dl_parts = [], []
        for p in range(hg // 2):
            j = g * (hg // 2) + p
            lanes = slice(j * LANES, (j + 1) * LANES)
            x_p = abuf[j]
            x16 = x_p.astype(BF16)
            ys, ecs, wend = [], [], []
            for e in range(2):
                h = 2 * j + e
                col = jnp.broadcast_to(cs[:, h:h + 1], (Q, LANES))
                row = cs_t[h:h + 1, :]
                dtr = dt_t[h:h + 1, :]
                tl = jnp.exp2(jnp.where(tri, col[:H] - row[:, :H], NEG_INF)) * (cb[:H, :H] * dtr[:, :H])
                bl = jnp.exp2(col[H:] - row[:, :H]) * (cb[H:, :H] * dtr[:, :H])
                br = jnp.exp2(jnp.where(tri, col[H:] - row[:, H:], NEG_INF)) * (cb[H:, H:] * dtr[:, H:])
                y_top = _dot(tl.astype(BF16), x16[:H])
                y_bot = _dot(jnp.concatenate([bl, br], axis=1).astype(BF16), x16)
                ys.append(jnp.concatenate([y_top, y_bot], axis=0))
                ecs.append(jnp.exp2(col))
                wend.append(jnp.exp2(jnp.broadcast_to(lw[:, h:h + 1], (Q, LANES))))
            y_p = (jnp.where(low, ys[0], ys[1])
                   + y_inter[:, p * LANES:(p + 1) * LANES] * jnp.where(low, ecs[0], ecs[1])
                   + dsk_ref[:, lanes] * x_p)
            y_ref[0, :, lanes] = y_p.astype(y_ref.dtype)
            xw_parts.append((x_p * jnp.where(low, wend[0], wend[1])).astype(BF16))
            dl_parts.append(jnp.where(low1, dec_last[:, 2 * j:2 * j + 1], dec_last[:, 2 * j + 1:2 * j + 2]))
        xw = jnp.concatenate(xw_parts, axis=1)
        dl = jnp.concatenate(dl_parts, axis=1)
        state[g] = st * dl + _dot(b_t, xw)


def _ssd_scan(xbc, dt, conv_w, conv_b, dt_bias, a_log, d_skip):
    B, S, _ = xbc.shape
    Q = CHUNK
    hg = SSD_HEADS // SSD_GROUPS
    return pl.pallas_call(
        _ssd_scan_kernel,
        grid=(B, S // Q),
        in_specs=[pl.BlockSpec((1, Q, XBC_DIM), lambda b, c: (b, c, 0)),
                  pl.BlockSpec((1, Q, LANES), lambda b, c: (b, c, 0)),
                  _full((D_CONV, XBC_DIM)), _full((1, XBC_DIM)), _full((1, LANES)), _full((1, LANES)),
                  _full((1, D_INNER))],
        out_specs=pl.BlockSpec((1, Q, D_INNER), lambda b, c: (b, c, 0)),
        out_shape=jax.ShapeDtypeStruct((B, S, D_INNER), BF16),
        scratch_shapes=[pltpu.VMEM((XBC_DIM // LANES, Q + 8, LANES), F32),
                        pltpu.VMEM((XBC_DIM // LANES, Q, LANES), F32),
                        pltpu.VMEM((SSD_GROUPS, D_STATE, hg * SSD_HEADDIM), F32)],
        compiler_params=_params("parallel", "arbitrary"),
        name="ssd_scan",
    )(xbc, dt, conv_w, conv_b, dt_bias, a_log, d_skip)


def _ssd_post_kernel(y_ref, z_ref, gn_ref, x_ref, wo_ref, *tail_refs):
    gsz = D_INNER // SSD_GROUPS
    out = None
    for g in range(SSD_GROUPS):
        sl = slice(g * gsz, (g + 1) * gsz)
        z = z_ref[:, sl].astype(F32)
        yz = y_ref[:, sl].astype(F32) * (z * _sigmoid(z))
        part = _dot(_rms(yz, gn_ref[:, sl]).astype(BF16), wo_ref[sl, :])
        out = part if out is None else out + part
    _route_tail(x_ref[...] + out, *tail_refs)


def _ssd_post(y, z, gn, x2, wo, tail):
    T = x2.shape[0]
    tm = min(ROW_TILE, T)
    row = lambda n: pl.BlockSpec((tm, n), lambda i: (i, 0))
    t_in, t_out = _tail_specs(tm)
    return pl.pallas_call(
        _ssd_post_kernel,
        grid=(T // tm,),
        in_specs=[row(D_INNER), row(D_INNER), _full((1, D_INNER)), row(D_MODEL), _full(wo.shape)] + t_in,
        out_specs=t_out,
        out_shape=_tail_shapes(T),
        scratch_shapes=[pltpu.VMEM((1, LANES), F32)],
        compiler_params=_params("arbitrary"),
        name="ssd_post",
    )(y, z, gn, x2, wo, *tail)


def _rope_tables(positions):
    pos = positions.reshape(-1, 1).astype(F32)
    lane = jnp.arange(LANES)

    def tables(rot, lead, period):
        half = rot // 2
        off = lane % period - lead
        rotary = (off >= 0) & (off < rot)
        inv = ROPE_THETA ** (-(2 * (off % half)).astype(F32) / rot)
        ang = pos * jnp.where(rotary, inv, 0.0)[None, :]
        sign = jnp.where(rotary, jnp.where(off < half, -1.0, 1.0), 0.0)
        return [jnp.cos(ang), jnp.sin(ang) * sign[None, :]]

    return tables(ROPE_DIM_A, 0, HEAD_DIM_A) + tables(MLA_ROPE, MLA_NOPE, LANES)


def _pad_cols(w, n):
    return jnp.pad(w, ((0, 0), (0, n - w.shape[1])))


def _head_pad(w, width):
    K = w.shape[0]
    w = w.reshape(K, -1, width)
    return jnp.pad(w, ((0, 0), (0, 0), (0, LANES - width))).reshape(K, -1)


def _router_weights(w_group, b_group, w_router, b_router):
    w = _pad_cols(jnp.concatenate([w_router, w_group], axis=1), LANES)
    b = _pad_cols(jnp.concatenate([b_router, b_group])[None, :], LANES)
    hi = w.astype(BF16)
    lo = (w - hi.astype(F32)).astype(BF16)
    return jnp.concatenate([hi, lo], axis=1), b


def kernel(x, positions, attn_norm, w_in_attn, mla_q_norm, w_uq, mla_kv_norm, w_ukv, w_out_attn,
           ssd_norm, w_in_ssd, conv_w, conv_b, dt_bias, a_log, d_skip, gate_norm, w_out_ssd,
           moe_norm, w_group, b_group, w_router, b_router, w_gate, w_up, w_down, final_norm):
    B, S, D = x.shape
    T = B * S
    x2 = x.reshape(T, D)
    tabs = _rope_tables(positions)

    def tail_params(layer):
        wr, b = _router_weights(w_group[layer], b_group[layer], w_router[layer], b_router[layer])
        return [moe_norm[layer][None, :], wr, b]

    w_in = w_in_attn[0]
    kr_off = 3 * MIX_A + MLA_Q_RANK + MLA_KV_RANK
    kr_cols = jnp.pad(w_in[:, kr_off:], ((0, 0), (MLA_NOPE, LANES - MLA_NOPE - MLA_ROPE)))
    w_cat = jnp.concatenate([w_in[:, :kr_off], kr_cols], axis=1).astype(BF16)
    wuq = _head_pad(w_uq[0], MLA_NOPE + MLA_ROPE).astype(BF16)
    ukv = w_ukv[0].reshape(MLA_KV_RANK, MLA_HEADS, MLA_NOPE + MLA_V)
    wuk = _head_pad(ukv[:, :, :MLA_NOPE].reshape(MLA_KV_RANK, -1), MLA_NOPE).astype(BF16)
    wuv = _head_pad(ukv[:, :, MLA_NOPE:].reshape(MLA_KV_RANK, -1), MLA_V).astype(BF16)
    pre = _attn_pre(x2, attn_norm[0][None, :], w_cat, mla_q_norm[0][None, :], wuq,
                    mla_kv_norm[0][None, :], wuk, wuv, tabs, S)
    qm, km, vm = pre[9:]
    shp = lambda t: t.reshape(B, S, t.shape[-1])
    qkv = [[t.reshape(B, 1, S, MIX_A) for t in pre[0:3]], pre[3:6], pre[6:9]]
    prev = None
    for n, d in enumerate(DILATIONS):
        prev = _dilated_branch(*qkv[n], d, prev, n == len(DILATIONS) - 1)
    ob = _mla_attention(shp(qm), shp(km), shp(vm)).reshape(T, MIX_B)
    x1, h, ri, rg, cnt = _attn_post(prev[0], ob, x2, w_out_attn[0].astype(BF16), tail_params(0))
    yg = _moe_rows(h, ri, cnt, 0, w_gate, w_up, w_down)

    w_cat = _pad_cols(w_in_ssd[0], D_INNER + XBC_DIM + LANES).astype(BF16)
    x2, z, xbc, dt = _ssd_pre(x1, yg, rg, ssd_norm[0][None, :], w_cat)
    y = _ssd_scan(xbc.reshape(B, S, XBC_DIM), dt.reshape(B, S, LANES), conv_w[0], conv_b[0][None, :],
                  _pad_cols(dt_bias[0][None, :], LANES), _pad_cols(a_log[0][None, :], LANES),
                  jnp.repeat(d_skip[0], SSD_HEADDIM)[None, :])
    x1, h, ri, rg, cnt = _ssd_post(y.reshape(T, D_INNER), z, gate_norm[0][None, :], x2,
                                   w_out_ssd[0].astype(BF16), tail_params(1))
    yg = _moe_rows(h, ri, cnt, 1, w_gate, w_up, w_down)
    out = _moe_combine(x1, yg, rg, final_norm[None, :])
    return out.reshape(B, S, D)
```

```python
import functools

import jax
import jax.numpy as jnp
from jax import lax
from jax.experimental import pallas as pl
from jax.experimental.pallas import tpu as pltpu
from jax.experimental.pallas import tpu_sc as plsc

F32 = jnp.float32
BF16 = jnp.bfloat16

D_MODEL = 1024
HEADS_A = 8
HEAD_DIM_A = 64
ROPE_DIM_A = 16
DILATIONS = (1, 4, 16)
SPAN = 128
DIL_ROWS = 8192
MLA_HEADS = 8
MLA_NOPE = 64
MLA_ROPE = 32
MLA_V = 64
MLA_Q_RANK = 256
MLA_KV_RANK = 128
ROPE_THETA = 500000.0
MIX_A = HEADS_A * HEAD_DIM_A
MIX_B = MLA_HEADS * MLA_V
D_INNER = 2048
SSD_HEADDIM = 64
SSD_HEADS = 32
SSD_GROUPS = 4
D_STATE = 128
D_CONV = 4
CHUNK = 256
XBC_DIM = D_INNER + 2 * SSD_GROUPS * D_STATE
N_GROUPS = 4
EXPERTS_PER_GROUP = 8
N_EXPERTS = 32
D_EXPERT = 512
NORM_EPS = 1e-6

LANES = 128
VMEM_LIMIT = 56 * 1024 * 1024
ROW_TILE = 512
MOE_ROWS = 512
SC_CORES = 2
SC_WORKERS = 32
SC_CHUNK = 64
MLA_TQ = 2048
MLA_TK = 512
MLA_UNROLL = 2
NEG_INF = float("-inf")
LOG2E = 1.4426950408889634


def _params(*sem):
    return pltpu.CompilerParams(dimension_semantics=sem, vmem_limit_bytes=VMEM_LIMIT)


def _dot(a, b):
    return jnp.dot(a, b, preferred_element_type=F32)


def _dot_nt(a, b):
    return lax.dot_general(a, b, (((1,), (1,)), ((), ())), preferred_element_type=F32)


def _rms(x, g):
    return x * lax.rsqrt(jnp.mean(x * x, axis=-1, keepdims=True) + NORM_EPS) * g


def _sigmoid(x):
    return 1.0 / (1.0 + jnp.exp(-x))


def _rope(t, c, s, half, first):
    partner = jnp.where(first, pltpu.roll(t, LANES - half, 1), pltpu.roll(t, half, 1))
    return t * c + partner * s


def _pack_bf16_pairs(x):
    n = x.shape[1] // 2
    bits = lax.bitcast_convert_type(x.astype(BF16).astype(F32), jnp.uint32)
    return (bits[:, :n] >> 16) | (bits[:, n:] & jnp.uint32(0xFFFF0000))


def _unpack_bf16_pairs(w):
    lo = lax.bitcast_convert_type(w << 16, F32)
    hi = lax.bitcast_convert_type(w & jnp.uint32(0xFFFF0000), F32)
    return jnp.concatenate([lo, hi], axis=1)


def _full(shape):
    return pl.BlockSpec(shape, lambda *_: (0,) * len(shape))


def _attn_pre_kernel(x_ref, g_ref, w_ref, qn_ref, wuq_ref, kvn_ref, wuk_ref, wuv_ref,
                     ca_ref, sa_ref, cb_ref, sb_ref,
                     qa_ref, ka_ref, va_ref, q4_ref, k4_ref, v4_ref, q16_ref, k16_ref, v16_ref,
                     qm_ref, km_ref, vm_ref, lay):
    xn = _rms(x_ref[...], g_ref[...]).astype(BF16)
    ca, sa, cb, sb = ca_ref[...], sa_ref[...], cb_ref[...], sb_ref[...]
    half_a = ROPE_DIM_A // 2
    half_b = MLA_ROPE // 2
    lane = lax.broadcasted_iota(jnp.int32, ca.shape, 1)
    first_a = lane % HEAD_DIM_A < half_a
    first_b = (lane >= MLA_NOPE) & (lane < MLA_NOPE + half_b)

    q = _dot(xn, w_ref[:, 0:MIX_A]) * (HEAD_DIM_A ** -0.5 * LOG2E)
    k = _dot(xn, w_ref[:, MIX_A:2 * MIX_A])
    v = _dot(xn, w_ref[:, 2 * MIX_A:3 * MIX_A])
    for c in range(MIX_A // LANES):
        sl = slice(c * LANES, (c + 1) * LANES)
        lay[0, c] = _rope(q[:, sl], ca, sa, half_a, first_a)
        lay[1, c] = _rope(k[:, sl], ca, sa, half_a, first_a)
        lay[2, c] = v[:, sl]
    tm = x_ref.shape[0]
    for a, (nat, by4, by16) in enumerate(((qa_ref, q4_ref, q16_ref), (ka_ref, k4_ref, k16_ref),
                                          (va_ref, v4_ref, v16_ref))):
        for c in range(MIX_A // LANES):
            sl = slice(c * LANES, (c + 1) * LANES)
            nat[:, sl] = lay[a, c].astype(BF16)
            for d, ref in ((4, by4), (16, by16)):
                for r in range(d):
                    ref[0, r, :, sl] = lay[a, c, pl.ds(r, tm // d, stride=d), :].astype(BF16)

    o = 3 * MIX_A
    cq = _dot(xn, w_ref[:, o:o + MLA_Q_RANK])
    ckv = _dot(xn, w_ref[:, o + MLA_Q_RANK:o + MLA_Q_RANK + MLA_KV_RANK])
    kr = _dot(xn, w_ref[:, o + MLA_Q_RANK + MLA_KV_RANK:o + MLA_Q_RANK + MLA_KV_RANK + LANES])
    cqn = _rms(cq, qn_ref[...]).astype(BF16)
    kvn = _rms(ckv, kvn_ref[...]).astype(BF16)
    krr = _rope(kr, cb, sb, half_b, first_b)
    qm = _dot(cqn, wuq_ref[...]) * ((MLA_NOPE + MLA_ROPE) ** -0.5 * LOG2E)
    km = _dot(kvn, wuk_ref[...])
    for h in range(MLA_HEADS):
        sl = slice(h * LANES, (h + 1) * LANES)
        qm_ref[:, sl] = _rope(qm[:, sl], cb, sb, half_b, first_b).astype(BF16)
        km_ref[:, sl] = (km[:, sl] + krr).astype(BF16)
    vm = _dot(kvn, wuv_ref[...])
    low = lax.broadcasted_iota(jnp.int32, vm.shape, 1) % LANES < MLA_V
    vm_ref[...] = jnp.where(low, vm, 1.0).astype(BF16)


def _attn_pre(x2, g, w_cat, qn, wuq, kvn, wuk, wuv, tabs, seq):
    T = x2.shape[0]
    tm = min(ROW_TILE, T)
    nb = seq // tm
    row = lambda n: pl.BlockSpec((tm, n), lambda i: (i, 0))
    wide = MLA_HEADS * LANES
    by_residue = [pl.BlockSpec((1, d, tm // d, MIX_A), lambda i: (i // nb, 0, i % nb, 0))
                  for d in DILATIONS[1:]]
    residue_shapes = [jax.ShapeDtypeStruct((T // seq, d, seq // d, MIX_A), BF16) for d in DILATIONS[1:]]
    return pl.pallas_call(
        _attn_pre_kernel,
        grid=(T // tm,),
        in_specs=[row(D_MODEL), _full((1, D_MODEL)), _full(w_cat.shape), _full((1, MLA_Q_RANK)),
                  _full(wuq.shape), _full((1, MLA_KV_RANK)), _full(wuk.shape), _full(wuv.shape)]
                 + [row(LANES)] * 4,
        out_specs=[row(MIX_A)] * 3 + [by_residue[0]] * 3 + [by_residue[1]] * 3 + [row(wide)] * 3,
        out_shape=[jax.ShapeDtypeStruct((T, MIX_A), BF16)] * 3 + [residue_shapes[0]] * 3
                  + [residue_shapes[1]] * 3 + [jax.ShapeDtypeStruct((T, wide), BF16)] * 3,
        scratch_shapes=[pltpu.VMEM((3, MIX_A // LANES, tm, LANES), F32)],
        compiler_params=_params("parallel"),
        name="attn_pre",
    )(x2, g, w_cat, qn, wuq, kvn, wuk, wuv, *tabs)


def _dilated_kernel(*refs, d, tq, carry, last):
    refs = list(refs)
    q_ref, kc_ref, kp_ref, vc_ref, vp_ref = refs[:5]
    op_ref, lp_ref = refs[5:7] if carry else (None, None)
    kbuf, vbuf = refs[-2:]
    o_ref = refs[7] if carry else refs[5]
    lse_ref = None if last else refs[-3]
    j = pl.program_id(2)
    for r in range(d):
        kbuf[r, 0:SPAN, :] = kp_ref[0, r]
        kbuf[r, SPAN:, :] = kc_ref[0, r]
        vbuf[r, 0:SPAN, :] = vp_ref[0, r]
        vbuf[r, SPAN:, :] = vc_ref[0, r]
    low = lax.broadcasted_iota(jnp.int32, (SPAN, LANES), 1) < HEAD_DIM_A
    qi = lax.broadcasted_iota(jnp.int32, (SPAN, 2 * SPAN), 0)
    ki = lax.broadcasted_iota(jnp.int32, (SPAN, 2 * SPAN), 1)
    band = (ki >= qi) & (ki <= qi + SPAN)
    first_valid = band & ((ki >= SPAN) | (j > 0))
    for r in range(d):
        for sub in range(tq // SPAN):
            valid = first_valid if sub == 0 else band
            q2 = q_ref[0, r, sub * SPAN:(sub + 1) * SPAN, :]
            kk = kbuf[r, sub * SPAN:(sub + 2) * SPAN, :]
            vv = vbuf[r, sub * SPAN:(sub + 2) * SPAN, :]
            os_, ls_ = [], []
            for e in range(2):
                qe = jnp.where(low if e == 0 else jnp.logical_not(low), q2, jnp.zeros_like(q2))
                s = jnp.where(valid, _dot_nt(qe, kk), NEG_INF)
                m = jnp.max(s, axis=-1, keepdims=True)
                pr = jnp.exp2(s - m)
                l = jnp.sum(pr, axis=-1, keepdims=True)
                os_.append(_dot(pr.astype(BF16), vv) / l)
                ls_.append(m + jnp.log(l) * LOG2E)
            o2 = jnp.where(low, os_[0], os_[1])
            l2 = jnp.where(low, ls_[0], ls_[1])
            rows = pl.ds(sub * SPAN, SPAN) if d == 1 else pl.ds(d * sub * SPAN + r, SPAN, stride=d)
            if carry:
                lp = lp_ref[0, 0, rows, :]
                mx = jnp.maximum(lp, l2)
                wp = jnp.exp2(lp - mx)
                wc = jnp.exp2(l2 - mx)
                den = wp + wc
                o2 = (wp * op_ref[0, 0, rows, :] + wc * o2) / den
                l2 = mx + jnp.log(den) * LOG2E
            o_ref[0, 0, rows, :] = o2
            if not last:
                lse_ref[0, 0, rows, :] = l2


def _dilated_branch(q, k, v, d, prev, last):
    B, _, L, _ = q.shape
    pairs = MIX_A // LANES
    rows = min(DIL_ROWS, d * L)
    tq = rows // d
    cur = pl.BlockSpec((1, d, tq, LANES), lambda b, p, j: (b, 0, j, p))
    prv = pl.BlockSpec((1, d, SPAN, LANES), lambda b, p, j: (b, 0, jnp.maximum(j * (tq // SPAN) - 1, 0), p))
    nat = pl.BlockSpec((1, 1, rows, LANES), lambda b, p, j: (b, p, j, 0))
    nat_shape = jax.ShapeDtypeStruct((B, pairs, d * L, LANES), F32)
    carry = prev is not None
    outs = pl.pallas_call(
        functools.partial(_dilated_kernel, d=d, tq=tq, carry=carry, last=last),
        grid=(B, pairs, L // tq),
        in_specs=[cur, cur, prv, cur, prv] + ([nat, nat] if carry else []),
        out_specs=[nat] if last else [nat, nat],
        out_shape=[nat_shape] if last else [nat_shape, nat_shape],
        scratch_shapes=[pltpu.VMEM((d, tq + SPAN, LANES), BF16), pltpu.VMEM((d, tq + SPAN, LANES), BF16)],
        compiler_params=_params("parallel", "parallel", "parallel"),
        name=f"dilated_d{d}",
    )(q, k, k, v, v, *(prev if carry else ()))
    return outs


def _mla_kernel(q_ref, k_ref, v_ref, o_ref, m0, m1, a0, a1, *, tq, tk):
    qi = pl.program_id(2)
    ms, accs = (m0, m1), (a0, a1)
    for e in range(2):
        ms[e][...] = jnp.full(ms[e].shape, NEG_INF, F32)
        accs[e][...] = jnp.zeros(accs[e].shape, F32)

    def step(kv, diag):
        start = pl.multiple_of(kv * tk, tk)
        r0 = 0 if diag is None else diag * tk
        for e in range(2):
            lanes = slice(e * LANES, (e + 1) * LANES)
            s = _dot_nt(q_ref[0, r0:, lanes], k_ref[0, pl.ds(start, tk), lanes])
            if diag is not None:
                row = lax.broadcasted_iota(jnp.int32, s.shape, 0)
                col = lax.broadcasted_iota(jnp.int32, s.shape, 1)
                s = jnp.where(col <= row, s, NEG_INF)
            m_prev = ms[e][r0:, :]
            m_new = jnp.maximum(m_prev, jnp.max(s, axis=-1, keepdims=True))
            alpha = jnp.exp2(m_prev - m_new)
            pr = jnp.concatenate([jnp.exp2(s[:, c * LANES:(c + 1) * LANES] - m_new)
                                  for c in range(tk // LANES)], axis=1).astype(BF16)
            accs[e][r0:, :] = alpha * accs[e][r0:, :] + _dot(pr, v_ref[0, pl.ds(start, tk), lanes])
            ms[e][r0:, :] = m_new

    n_full = qi * (tq // tk)
    n_main = n_full // MLA_UNROLL

    def body(i, c):
        for u in range(MLA_UNROLL):
            step(MLA_UNROLL * i + u, None)
        return c

    def single(i, c):
        step(i, None)
        return c

    lax.fori_loop(0, n_main, body, 0)
    lax.fori_loop(n_main * MLA_UNROLL, n_full, single, 0)

    for t in range(tq // tk):
        step(n_full + t, t)
    low = lax.broadcasted_iota(jnp.int32, (tq, LANES), 1) < MLA_V
    r0 = a0[...] / pltpu.roll(a0[...], MLA_V, 1)
    r1 = a1[...] / pltpu.roll(a1[...], MLA_V, 1)
    o_ref[0] = jnp.where(low, r0, pltpu.roll(r1, MLA_V, 1)).astype(BF16)


def _mla_attention(qm, km, vm):
    B, S, _ = qm.shape
    tq = min(MLA_TQ, S)
    tk = min(MLA_TK, tq)
    pairs = MLA_HEADS // 2
    return pl.pallas_call(
        functools.partial(_mla_kernel, tq=tq, tk=tk),
        grid=(B, pairs, S // tq),
        in_specs=[pl.BlockSpec((1, tq, 2 * LANES), lambda b, p, i: (b, i, p)),
                  pl.BlockSpec((1, S, 2 * LANES), lambda b, p, i: (b, 0, p)),
                  pl.BlockSpec((1, S, 2 * LANES), lambda b, p, i: (b, 0, p))],
        out_specs=pl.BlockSpec((1, tq, LANES), lambda b, p, i: (b, i, p)),
        out_shape=jax.ShapeDtypeStruct((B, S, MIX_B), BF16),
        scratch_shapes=[pltpu.VMEM((tq, LANES), F32)] * 4,
        compiler_params=_params("parallel", "parallel", "arbitrary"),
        name="mla_flash",
    )(qm, km, vm)


def _route_tail(x1, gm_ref, wr_ref, br_ref, x1_ref, h_ref, ri_ref, rg_ref, cnt_ref, cnt_sc):
    x1_ref[...] = x1
    hn = _rms(x1, gm_ref[...])
    h_hi = hn.astype(BF16)
    h_ref[...] = _pack_bf16_pairs(hn)
    h_lo = (hn - h_hi.astype(F32)).astype(BF16)
    both = _dot(h_hi, wr_ref[...])
    logits = both[:, :LANES] + both[:, LANES:] + _dot(h_lo, wr_ref[:, 0:LANES]) + br_ref[...]
    tm = logits.shape[0]
    lane = lax.broadcasted_iota(jnp.int32, (tm, LANES), 1)
    lanef = lane.astype(F32)
    big = float(LANES)
    is_g = (lane >= N_EXPERTS) & (lane < N_EXPERTS + N_GROUPS)
    gl = jnp.where(is_g, logits, NEG_INF)
    gmax = jnp.max(gl, axis=-1, keepdims=True)
    gsum = jnp.sum(jnp.exp(gl - gmax), axis=-1, keepdims=True)
    g_val = 1.0 / gsum
    g_idx = jnp.min(jnp.where(gl == gmax, lanef, big), axis=-1, keepdims=True) - float(N_EXPERTS)
    in_grp = (lane < N_EXPERTS) & ((lane // EXPERTS_PER_GROUP).astype(F32) == g_idx)
    el = jnp.where(in_grp, logits, NEG_INF)
    e1 = jnp.max(el, axis=-1, keepdims=True)
    i1 = jnp.min(jnp.where(el == e1, lanef, big), axis=-1, keepdims=True)
    el2 = jnp.where(lanef == i1, NEG_INF, el)
    e2 = jnp.max(el2, axis=-1, keepdims=True)
    i2 = jnp.min(jnp.where(el2 == e2, lanef, big), axis=-1, keepdims=True)
    t = jnp.exp(e2 - e1)
    p1 = 1.0 / (1.0 + t)
    rg_ref[...] = jnp.where(lane == 0, p1 * g_val, jnp.where(lane == 1, t * p1 * g_val, 0.0))
    @pl.when(pl.program_id(0) == 0)
    def _():
        cnt_sc[...] = jnp.zeros(cnt_sc.shape, F32)

    oh1 = lanef == i1
    oh2 = lanef == i2
    oh = jnp.where(oh1 | oh2, 1.0, 0.0)
    rr = lax.broadcasted_iota(jnp.int32, (tm, tm), 0)
    cc = lax.broadcasted_iota(jnp.int32, (tm, tm), 1)
    strict = jnp.where(cc < rr, 1.0, 0.0).astype(BF16)
    before = _dot(strict, oh.astype(BF16)) + cnt_sc[...]
    rank1 = jnp.sum(jnp.where(oh1, before, 0.0), axis=-1, keepdims=True)
    rank2 = jnp.sum(jnp.where(oh2, before, 0.0), axis=-1, keepdims=True)
    cnt_sc[...] = cnt_sc[...] + jnp.sum(oh, axis=0, keepdims=True)
    cnt_ref[...] = jnp.broadcast_to(cnt_sc[...], cnt_ref.shape).astype(jnp.int32)
    ri_ref[...] = jnp.where(lane == 0, i1, jnp.where(lane == 1, i2, jnp.where(
        lane == 2, rank1, jnp.where(lane == 3, rank2, 0.0)))).astype(jnp.int32)


def _tail_specs(tm):
    row = lambda n: pl.BlockSpec((tm, n), lambda i: (i, 0))
    in_specs = [_full((1, D_MODEL)), _full((D_MODEL, 2 * LANES)), _full((1, LANES))]
    out_specs = [row(D_MODEL), row(D_MODEL // 2), row(LANES), row(LANES), _full((8, LANES))]
    return in_specs, out_specs


def _tail_shapes(T):
    return [jax.ShapeDtypeStruct((T, D_MODEL), F32), jax.ShapeDtypeStruct((T, D_MODEL // 2), jnp.uint32),
            jax.ShapeDtypeStruct((T, LANES), jnp.int32), jax.ShapeDtypeStruct((T, LANES), F32),
            jax.ShapeDtypeStruct((8, LANES), jnp.int32)]


def _attn_post_kernel(oa_ref, ob_ref, x_ref, wo_ref, *tail_refs):
    oa = jnp.concatenate([oa_ref[0, p] for p in range(MIX_A // LANES)], axis=1).astype(BF16)
    y = _dot(oa, wo_ref[0:MIX_A, :]) + _dot(ob_ref[...], wo_ref[MIX_A:, :])
    _route_tail(x_ref[...] + y, *tail_refs)


def _attn_post(oa, ob, x2, wo, tail):
    T = x2.shape[0]
    tm = min(ROW_TILE, T)
    nb = oa.shape[2] // tm
    row = lambda n: pl.BlockSpec((tm, n), lambda i: (i, 0))
    t_in, t_out = _tail_specs(tm)
    return pl.pallas_call(
        _attn_post_kernel,
        grid=(T // tm,),
        in_specs=[pl.BlockSpec((1, MIX_A // LANES, tm, LANES), lambda i: (i // nb, 0, i % nb, 0)),
                  row(MIX_B), row(D_MODEL), _full(wo.shape)] + t_in,
        out_specs=t_out,
        out_shape=_tail_shapes(T),
        scratch_shapes=[pltpu.VMEM((1, LANES), F32)],
        compiler_params=_params("arbitrary"),
        name="attn_post",
    )(oa, ob, x2, wo, *tail)


def _expert_kernel(be_ref, nr_ref, xs_ref, wg_ref, wu_ref, wd_ref, y_ref, wg_bf, wu_bf, wd_bf):
    i = pl.program_id(0)

    @pl.when((i == 0) | (be_ref[i] != be_ref[jnp.maximum(i - 1, 0)]))
    def _():
        wg_bf[...] = wg_ref[0].astype(BF16)
        wu_bf[...] = wu_ref[0].astype(BF16)
        wd_bf[...] = wd_ref[0].astype(BF16)

    @pl.when(nr_ref[i] > 0)
    def _():
        live = lax.broadcasted_iota(jnp.int32, xs_ref.shape, 0) < nr_ref[i]
        xb = _unpack_bf16_pairs(jnp.where(live, xs_ref[...], jnp.uint32(0))).astype(BF16)
        g = _dot(xb, wg_bf[...])
        u = _dot(xb, wu_bf[...])
        hid = (g * _sigmoid(g) * u).astype(BF16)
        y_ref[...] = _pack_bf16_pairs(_dot(hid, wd_bf[...]))

    @pl.when(nr_ref[i] == 0)
    def _():
        y_ref[...] = jnp.zeros(y_ref.shape, y_ref.dtype)


def _expert_blocks(xs, block_expert, block_rows, layer, wg, wu, wd):
    rows = xs.shape[0]
    bm = MOE_ROWS
    half = D_MODEL // 2
    grid_spec = pltpu.PrefetchScalarGridSpec(
        num_scalar_prefetch=2,
        grid=(rows // bm,),
        in_specs=[pl.BlockSpec((bm, half), lambda i, be, nr: (i, 0)),
                  pl.BlockSpec((None, 1, D_MODEL, D_EXPERT), lambda i, be, nr: (layer, be[i], 0, 0)),
                  pl.BlockSpec((None, 1, D_MODEL, D_EXPERT), lambda i, be, nr: (layer, be[i], 0, 0)),
                  pl.BlockSpec((None, 1, D_EXPERT, D_MODEL), lambda i, be, nr: (layer, be[i], 0, 0))],
        out_specs=pl.BlockSpec((bm, half), lambda i, be, nr: (i, 0)),
        scratch_shapes=[pltpu.VMEM((D_MODEL, D_EXPERT), BF16), pltpu.VMEM((D_MODEL, D_EXPERT), BF16),
                        pltpu.VMEM((D_EXPERT, D_MODEL), BF16)],
    )
    return pl.pallas_call(
        _expert_kernel,
        grid_spec=grid_spec,
        out_shape=jax.ShapeDtypeStruct((rows, half), jnp.uint32),
        compiler_params=_params("arbitrary"),
        name="moe_experts",
    )(block_expert, block_rows, xs, wg, wu, wd)


def _moe_sum(x_ref, r0_ref, r1_ref, rg_ref):
    rg = rg_ref[...]
    return (x_ref[...] + rg[:, 0:1] * _unpack_bf16_pairs(r0_ref[...])
            + rg[:, 1:2] * _unpack_bf16_pairs(r1_ref[...]))


def _combine_kernel(x_ref, r0_ref, r1_ref, rg_ref, gf_ref, o_ref):
    o_ref[...] = _rms(_moe_sum(x_ref, r0_ref, r1_ref, rg_ref), gf_ref[...])


def _moe_combine(x1, yg, rg, gfinal):
    T = x1.shape[0]
    tm = min(ROW_TILE, T)
    half = D_MODEL // 2
    row = lambda n: pl.BlockSpec((tm, n), lambda i: (i, 0))
    return pl.pallas_call(
        _combine_kernel,
        grid=(T // tm,),
        in_specs=[row(D_MODEL), row(half), pl.BlockSpec((tm, half), lambda i: (i + T // tm, 0)),
                  row(LANES), _full((1, D_MODEL))],
        out_specs=row(D_MODEL),
        out_shape=jax.ShapeDtypeStruct((T, D_MODEL), F32),
        compiler_params=_params("parallel"),
        name="moe_combine",
    )(x1, yg, yg, rg, gfinal)


def _sc_rows(n_rows):
    per_worker = n_rows // SC_WORKERS
    assert n_rows % SC_WORKERS == 0 and per_worker % SC_CHUNK == 0
    return per_worker, per_worker // SC_CHUNK


def _sc_dispatch(hp, dest0, dest1, rows):
    T, W = hp.shape
    per_w, n_sub = _sc_rows(T)
    mesh = plsc.VectorSubcoreMesh(core_axis_name="c", subcore_axis_name="s")

    @functools.partial(
        pl.kernel, mesh=mesh,
        out_type=jax.ShapeDtypeStruct((rows, W), hp.dtype),
        scratch_types=[pltpu.VMEM((n_sub, SC_CHUNK), jnp.int32), pltpu.VMEM((n_sub, SC_CHUNK), jnp.int32),
                       pltpu.VMEM((SC_CHUNK, W), hp.dtype)],
    )
    def k(hp_hbm, d0_hbm, d1_hbm, out_hbm, d0_v, d1_v, buf):
        wid = lax.axis_index("s") * SC_CORES + lax.axis_index("c")
        pltpu.sync_copy(d0_hbm.at[wid], d0_v)
        pltpu.sync_copy(d1_hbm.at[wid], d1_v)

        @pl.loop(0, n_sub)
        def _(j):
            pltpu.sync_copy(hp_hbm.at[pl.ds(wid * per_w + j * SC_CHUNK, SC_CHUNK)], buf)
            pltpu.sync_copy(buf, out_hbm.at[d0_v.at[j]])
            pltpu.sync_copy(buf, out_hbm.at[d1_v.at[j]])

    shp = (SC_WORKERS, n_sub, SC_CHUNK)
    return k(hp, dest0.reshape(shp), dest1.reshape(shp))


def _sc_gather(table, idx):
    M = idx.shape[0]
    W = table.shape[1]
    per_w, n_sub = _sc_rows(M)
    assert n_sub % 2 == 0
    mesh = plsc.VectorSubcoreMesh(core_axis_name="c", subcore_axis_name="s")

    @functools.partial(
        pl.kernel, mesh=mesh,
        out_type=jax.ShapeDtypeStruct((M, W), table.dtype),
        scratch_types=[pltpu.VMEM((n_sub, SC_CHUNK), jnp.int32),
                       pltpu.VMEM((SC_CHUNK, W), table.dtype), pltpu.VMEM((SC_CHUNK, W), table.dtype),
                       pltpu.SemaphoreType.DMA, pltpu.SemaphoreType.DMA],
    )
    def k(table_hbm, idx_hbm, out_hbm, idx_v, buf0, buf1, sem0, sem1):
        wid = lax.axis_index("s") * SC_CORES + lax.axis_index("c")
        pltpu.sync_copy(idx_hbm.at[wid], idx_v)
        bufs, sems = (buf0, buf1), (sem0, sem1)

        def gather(j, slot):
            return pltpu.make_async_copy(table_hbm.at[idx_v.at[j]], bufs[slot], sems[slot])

        gather(0, 0).start()

        @pl.loop(0, n_sub // 2)
        def _(jj):
            for slot in range(2):
                j = jj * 2 + slot

                @pl.when(j + 1 < n_sub)
                def _():
                    gather(j + 1, 1 - slot).start()

                gather(j, slot).wait()
                pltpu.sync_copy(bufs[slot], out_hbm.at[pl.ds(wid * per_w + j * SC_CHUNK, SC_CHUNK)])

    return k(table, idx.reshape(SC_WORKERS, n_sub, SC_CHUNK))


def _moe_rows(hp, ri, cnt, layer, wg, wu, wd):
    T = hp.shape[0]
    bm = MOE_ROWS
    experts = jnp.arange(N_EXPERTS, dtype=jnp.int32)
    counts = cnt[0, :N_EXPERTS]
    padded = (counts + bm - 1) // bm * bm
    pad_ends = jnp.cumsum(padded)
    pad_starts = pad_ends - padded
    dest2 = ri[:, 2:4] + jnp.sum(jnp.where(ri[:, 0:2, None] > experts, padded, 0), axis=-1)
    n_blocks = (2 * T) // bm + N_EXPERTS
    block_start = jnp.arange(n_blocks, dtype=jnp.int32) * bm
    block_expert = jnp.minimum(jnp.sum((pad_ends[None, :] <= block_start[:, None]).astype(jnp.int32), axis=1),
                               N_EXPERTS - 1)
    block_rows = jnp.clip(counts[block_expert] - (block_start - pad_starts[block_expert]), 0, bm)
    xs = _sc_dispatch(hp, dest2[:, 0], dest2[:, 1], n_blocks * bm)
    y_rows = _expert_blocks(xs, block_expert, block_rows, layer, wg, wu, wd)
    return _sc_gather(y_rows, jnp.concatenate([dest2[:, 0], dest2[:, 1]]))


def _ssd_pre_kernel(x_ref, r0_ref, r1_ref, rg_ref, g_ref, w_ref, x2_ref, z_ref, xbc_ref, dt_ref):
    x2 = _moe_sum(x_ref, r0_ref, r1_ref, rg_ref)
    x2_ref[...] = x2
    xn = _rms(x2, g_ref[...]).astype(BF16)
    step = 512
    for c in range(0, D_INNER, step):
        z_ref[:, c:c + step] = _dot(xn, w_ref[:, c:c + step]).astype(BF16)
    for c in range(0, XBC_DIM, step):
        xbc_ref[:, c:c + step] = _dot(xn, w_ref[:, D_INNER + c:D_INNER + c + step]).astype(BF16)
    dt_ref[...] = _dot(xn, w_ref[:, D_INNER + XBC_DIM:])


def _ssd_pre(x1, yg, rg, g, w_cat):
    T = x1.shape[0]
    tm = min(ROW_TILE, T)
    half = D_MODEL // 2
    row = lambda n: pl.BlockSpec((tm, n), lambda i: (i, 0))
    return pl.pallas_call(
        _ssd_pre_kernel,
        grid=(T // tm,),
        in_specs=[row(D_MODEL), row(half), pl.BlockSpec((tm, half), lambda i: (i + T // tm, 0)), row(LANES),
                  _full((1, D_MODEL)), _full(w_cat.shape)],
        out_specs=[row(D_MODEL), row(D_INNER), row(XBC_DIM), row(LANES)],
        out_shape=[jax.ShapeDtypeStruct((T, D_MODEL), F32), jax.ShapeDtypeStruct((T, D_INNER), BF16),
                   jax.ShapeDtypeStruct((T, XBC_DIM), BF16), jax.ShapeDtypeStruct((T, LANES), F32)],
        compiler_params=_params("parallel"),
        name="ssd_pre",
    )(x1, yg, yg, rg, g, w_cat)


def _ssd_scan_kernel(xbc_ref, dt_ref, cw_ref, cb_ref, dtb_ref, alog_ref, dsk_ref, y_ref,
                     ubuf, abuf, state):
    Q = CHUNK
    H = Q // 2
    c = pl.program_id(1)
    n_blk = XBC_DIM // LANES

    @pl.when(c == 0)
    def _():
        ubuf[:, 0:8, :] = jnp.zeros((n_blk, 8, LANES), F32)
        state[...] = jnp.zeros(state.shape, F32)

    phases = 8
    for blk in range(n_blk):
        lanes = slice(blk * LANES, (blk + 1) * LANES)
        ubuf[blk, 8:8 + Q, :] = xbc_ref[0, :, lanes].astype(F32)
        taps = [cw_ref[k:k + 1, lanes] for k in range(D_CONV)]
        bias = cb_ref[:, lanes]
        for s in range(phases):
            acc = bias
            for k in range(D_CONV):
                acc = acc + taps[k] * ubuf[blk, pl.ds(8 - (D_CONV - 1) + k + s, Q // phases, stride=phases), :]
            abuf[blk, pl.ds(s, Q // phases, stride=phases), :] = acc * _sigmoid(acc)
        ubuf[blk, 0:8, :] = ubuf[blk, Q:Q + 8, :]

    x_dt = dt_ref[0] + dtb_ref[...]
    dt = jnp.maximum(x_dt, 0.0) + jnp.log(1.0 + jnp.exp(-jnp.abs(x_dt)))
    da = dt * (-jnp.exp(alog_ref[...]) * LOG2E)
    ri = lax.broadcasted_iota(jnp.int32, (Q, Q), 0)
    ci = lax.broadcasted_iota(jnp.int32, (Q, Q), 1)
    tril = jnp.where(ci <= ri, 1.0, 0.0).astype(BF16)
    d_hi = da.astype(BF16)
    r1 = da - d_hi.astype(F32)
    d_mid = r1.astype(BF16)
    d_lo = (r1 - d_mid.astype(F32)).astype(BF16)
    cs = _dot(tril, d_hi) + _dot(tril, d_mid) + _dot(tril, d_lo)
    cs_last = cs[Q - 1:Q, :]
    cs_t = jnp.transpose(cs)
    dt_t = jnp.transpose(dt)
    lw = jnp.log(dt) * LOG2E + (cs_last - cs)
    dec_last = jnp.exp2(cs_last)
    low = lax.broadcasted_iota(jnp.int32, (Q, LANES), 1) < SSD_HEADDIM
    low1 = low[0:1, :]
    tri = lax.broadcasted_iota(jnp.int32, (H, H), 1) <= lax.broadcasted_iota(jnp.int32, (H, H), 0)
    hg = SSD_HEADS // SSD_GROUPS
    x_blocks = D_INNER // LANES
    for g in range(SSD_GROUPS):
        b_g = abuf[x_blocks + g]
        c_g = abuf[x_blocks + SSD_GROUPS + g].astype(BF16)
        cb = _dot_nt(c_g, b_g.astype(BF16))
        b_t = jnp.transpose(b_g).astype(BF16)
        st = state[g]
        y_inter = _dot(c_g, st.astype(BF16))
        xw_parts, dl_parts = [], []
        for p in range(hg // 2):
            j = g * (hg // 2) + p
            lanes = slice(j * LANES, (j + 1) * LANES)
            x_p = abuf[j]
            x16 = x_p.astype(BF16)
            ys, ecs, wend = [], [], []
            for e in range(2):
                h = 2 * j + e
                col = jnp.broadcast_to(cs[:, h:h + 1], (Q, LANES))
                row = cs_t[h:h + 1, :]
                dtr = dt_t[h:h + 1, :]
                tl = jnp.exp2(jnp.where(tri, col[:H] - row[:, :H], NEG_INF)) * (cb[:H, :H] * dtr[:, :H])
                bl = jnp.exp2(col[H:] - row[:, :H]) * (cb[H:, :H] * dtr[:, :H])
                br = jnp.exp2(jnp.where(tri, col[H:] - row[:, H:], NEG_INF)) * (cb[H:, H:] * dtr[:, H:])
                y_top = _dot(tl.astype(BF16), x16[:H])
                y_bot = _dot(jnp.concatenate([bl, br], axis=1).astype(BF16), x16)
                ys.append(jnp.concatenate([y_top, y_bot], axis=0))
                ecs.append(jnp.exp2(col))
                wend.append(jnp.exp2(jnp.broadcast_to(lw[:, h:h + 1], (Q, LANES))))
            y_p = (jnp.where(low, ys[0], ys[1])
                   + y_inter[:, p * LANES:(p + 1) * LANES] * jnp.where(low, ecs[0], ecs[1])
                   + dsk_ref[:, lanes] * x_p)
            y_ref[0, :, lanes] = y_p.astype(y_ref.dtype)
            xw_parts.append((x_p * jnp.where(low, wend[0], wend[1])).astype(BF16))
            dl_parts.append(jnp.where(low1, dec_last[:, 2 * j:2 * j + 1], dec_last[:, 2 * j + 1:2 * j + 2]))
        xw = jnp.concatenate(xw_parts, axis=1)
        dl = jnp.concatenate(dl_parts, axis=1)
        state[g] = st * dl + _dot(b_t, xw)


def _ssd_scan(xbc, dt, conv_w, conv_b, dt_bias, a_log, d_skip):
    B, S, _ = xbc.shape
    Q = CHUNK
    hg = SSD_HEADS // SSD_GROUPS
    return pl.pallas_call(
        _ssd_scan_kernel,
        grid=(B, S // Q),
        in_specs=[pl.BlockSpec((1, Q, XBC_DIM), lambda b, c: (b, c, 0)),
                  pl.BlockSpec((1, Q, LANES), lambda b, c: (b, c, 0)),
                  _full((D_CONV, XBC_DIM)), _full((1, XBC_DIM)), _full((1, LANES)), _full((1, LANES)),
                  _full((1, D_INNER))],
        out_specs=pl.BlockSpec((1, Q, D_INNER), lambda b, c: (b, c, 0)),
        out_shape=jax.ShapeDtypeStruct((B, S, D_INNER), BF16),
        scratch_shapes=[pltpu.VMEM((XBC_DIM // LANES, Q + 8, LANES), F32),
                        pltpu.VMEM((XBC_DIM // LANES, Q, LANES), F32),
                        pltpu.VMEM((SSD_GROUPS, D_STATE, hg * SSD_HEADDIM), F32)],
        compiler_params=_params("parallel", "arbitrary"),
        name="ssd_scan",
    )(xbc, dt, conv_w, conv_b, dt_bias, a_log, d_skip)


def _ssd_post_kernel(y_ref, z_ref, gn_ref, x_ref, wo_ref, *tail_refs):
    gsz = D_INNER // SSD_GROUPS
    out = None
    for g in range(SSD_GROUPS):
        sl = slice(g * gsz, (g + 1) * gsz)
        z = z_ref[:, sl].astype(F32)
        yz = y_ref[:, sl].astype(F32) * (z * _sigmoid(z))
        part = _dot(_rms(yz, gn_ref[:, sl]).astype(BF16), wo_ref[sl, :])
        out = part if out is None else out + part
    _route_tail(x_ref[...] + out, *tail_refs)


def _ssd_post(y, z, gn, x2, wo, tail):
    T = x2.shape[0]
    tm = min(ROW_TILE, T)
    row = lambda n: pl.BlockSpec((tm, n), lambda i: (i, 0))
    t_in, t_out = _tail_specs(tm)
    return pl.pallas_call(
        _ssd_post_kernel,
        grid=(T // tm,),
        in_specs=[row(D_INNER), row(D_INNER), _full((1, D_INNER)), row(D_MODEL), _full(wo.shape)] + t_in,
        out_specs=t_out,
        out_shape=_tail_shapes(T),
        scratch_shapes=[pltpu.VMEM((1, LANES), F32)],
        compiler_params=_params("arbitrary"),
        name="ssd_post",
    )(y, z, gn, x2, wo, *tail)


def _rope_tables(positions):
    pos = positions.reshape(-1, 1).astype(F32)
    lane = jnp.arange(LANES)

    def tables(rot, lead, period):
        half = rot // 2
        off = lane % period - lead
        rotary = (off >= 0) & (off < rot)
        inv = ROPE_THETA ** (-(2 * (off % half)).astype(F32) / rot)
        ang = pos * jnp.where(rotary, inv, 0.0)[None, :]
        sign = jnp.where(rotary, jnp.where(off < half, -1.0, 1.0), 0.0)
        return [jnp.cos(ang), jnp.sin(ang) * sign[None, :]]

    return tables(ROPE_DIM_A, 0, HEAD_DIM_A) + tables(MLA_ROPE, MLA_NOPE, LANES)


def _pad_cols(w, n):
    return jnp.pad(w, ((0, 0), (0, n - w.shape[1])))


def _head_pad(w, width):
    K = w.shape[0]
    w = w.reshape(K, -1, width)
    return jnp.pad(w, ((0, 0), (0, 0), (0, LANES - width))).reshape(K, -1)


def _router_weights(w_group, b_group, w_router, b_router):
    w = _pad_cols(jnp.concatenate([w_router, w_group], axis=1), LANES)
    b = _pad_cols(jnp.concatenate([b_router, b_group])[None, :], LANES)
    hi = w.astype(BF16)
    lo = (w - hi.astype(F32)).astype(BF16)
    return jnp.concatenate([hi, lo], axis=1), b


def kernel(x, positions, attn_norm, w_in_attn, mla_q_norm, w_uq, mla_kv_norm, w_ukv, w_out_attn,
           ssd_norm, w_in_ssd, conv_w, conv_b, dt_bias, a_log, d_skip, gate_norm, w_out_ssd,
           moe_norm, w_group, b_group, w_router, b_router, w_gate, w_up, w_down, final_norm):
    B, S, D = x.shape
    T = B * S
    x2 = x.reshape(T, D)
    tabs = _rope_tables(positions)

    def tail_params(layer):
        wr, b = _router_weights(w_group[layer], b_group[layer], w_router[layer], b_router[layer])
        return [moe_norm[layer][None, :], wr, b]

    w_in = w_in_attn[0]
    kr_off = 3 * MIX_A + MLA_Q_RANK + MLA_KV_RANK
    kr_cols = jnp.pad(w_in[:, kr_off:], ((0, 0), (MLA_NOPE, LANES - MLA_NOPE - MLA_ROPE)))
    w_cat = jnp.concatenate([w_in[:, :kr_off], kr_cols], axis=1).astype(BF16)
    wuq = _head_pad(w_uq[0], MLA_NOPE + MLA_ROPE).astype(BF16)
    ukv = w_ukv[0].reshape(MLA_KV_RANK, MLA_HEADS, MLA_NOPE + MLA_V)
    wuk = _head_pad(ukv[:, :, :MLA_NOPE].reshape(MLA_KV_RANK, -1), MLA_NOPE).astype(BF16)
    wuv = _head_pad(ukv[:, :, MLA_NOPE:].reshape(MLA_KV_RANK, -1), MLA_V).astype(BF16)
    pre = _attn_pre(x2, attn_norm[0][None, :], w_cat, mla_q_norm[0][None, :], wuq,
                    mla_kv_norm[0][None, :], wuk, wuv, tabs, S)
    qm, km, vm = pre[9:]
    shp = lambda t: t.reshape(B, S, t.shape[-1])
    qkv = [[t.reshape(B, 1, S, MIX_A) for t in pre[0:3]], pre[3:6], pre[6:9]]
    prev = None
    for n, d in enumerate(DILATIONS):
        prev = _dilated_branch(*qkv[n], d, prev, n == len(DILATIONS) - 1)
    ob = _mla_attention(shp(qm), shp(km), shp(vm)).reshape(T, MIX_B)
    x1, h, ri, rg, cnt = _attn_post(prev[0], ob, x2, w_out_attn[0].astype(BF16), tail_params(0))
    yg = _moe_rows(h, ri, cnt, 0, w_gate, w_up, w_down)

    w_cat = _pad_cols(w_in_ssd[0], D_INNER + XBC_DIM + LANES).astype(BF16)
    x2, z, xbc, dt = _ssd_pre(x1, yg, rg, ssd_norm[0][None, :], w_cat)
    y = _ssd_scan(xbc.reshape(B, S, XBC_DIM), dt.reshape(B, S, LANES), conv_w[0], conv_b[0][None, :],
                  _pad_cols(dt_bias[0][None, :], LANES), _pad_cols(a_log[0][None, :], LANES),
                  jnp.repeat(d_skip[0], SSD_HEADDIM)[None, :])
    x1, h, ri, rg, cnt = _ssd_post(y.reshape(T, D_INNER), z, gate_norm[0][None, :], x2,
                                   w_out_ssd[0].astype(BF16), tail_params(1))
    yg = _moe_rows(h, ri, cnt, 1, w_gate, w_up, w_down)
    out = _moe_combine(x1, yg, rg, final_norm[None, :])
    return out.reshape(B, S, D)
```

```python
import functools

import jax
import jax.numpy as jnp
from jax import lax
from jax.experimental import pallas as pl
from jax.experimental.pallas import tpu as pltpu
from jax.experimental.pallas import tpu_sc as plsc

F32 = jnp.float32
BF16 = jnp.bfloat16

D_MODEL = 1024
HEADS_A = 8
HEAD_DIM_A = 64
ROPE_DIM_A = 16
DILATIONS = (1, 4, 16)
SPAN = 128
DIL_ROWS = 8192
MLA_HEADS = 8
MLA_NOPE = 64
MLA_ROPE = 32
MLA_V = 64
MLA_Q_RANK = 256
MLA_KV_RANK = 128
ROPE_THETA = 500000.0
MIX_A = HEADS_A * HEAD_DIM_A
MIX_B = MLA_HEADS * MLA_V
D_INNER = 2048
SSD_HEADDIM = 64
SSD_HEADS = 32
SSD_GROUPS = 4
D_STATE = 128
D_CONV = 4
CHUNK = 256
XBC_DIM = D_INNER + 2 * SSD_GROUPS * D_STATE
N_GROUPS = 4
EXPERTS_PER_GROUP = 8
N_EXPERTS = 32
D_EXPERT = 512
NORM_EPS = 1e-6

LANES = 128
VMEM_LIMIT = 56 * 1024 * 1024
ROW_TILE = 512
MOE_ROWS = 512
SC_CORES = 2
SC_WORKERS = 32
SC_CHUNK = 64
MLA_TQ = 2048
MLA_TK = 512
MLA_UNROLL = 2
NEG_INF = float("-inf")
LOG2E = 1.4426950408889634


def _params(*sem):
    return pltpu.CompilerParams(dimension_semantics=sem, vmem_limit_bytes=VMEM_LIMIT)


def _dot(a, b):
    return jnp.dot(a, b, preferred_element_type=F32)


def _dot_nt(a, b):
    return lax.dot_general(a, b, (((1,), (1,)), ((), ())), preferred_element_type=F32)


def _rms(x, g):
    return x * lax.rsqrt(jnp.mean(x * x, axis=-1, keepdims=True) + NORM_EPS) * g


def _sigmoid(x):
    return 1.0 / (1.0 + jnp.exp(-x))


def _rope(t, c, s, half, first):
    partner = jnp.where(first, pltpu.roll(t, LANES - half, 1), pltpu.roll(t, half, 1))
    return t * c + partner * s


def _pack_bf16_pairs(x):
    n = x.shape[1] // 2
    bits = lax.bitcast_convert_type(x.astype(BF16).astype(F32), jnp.uint32)
    return (bits[:, :n] >> 16) | (bits[:, n:] & jnp.uint32(0xFFFF0000))


def _unpack_bf16_pairs(w):
    lo = lax.bitcast_convert_type(w << 16, F32)
    hi = lax.bitcast_convert_type(w & jnp.uint32(0xFFFF0000), F32)
    return jnp.concatenate([lo, hi], axis=1)


def _full(shape):
    return pl.BlockSpec(shape, lambda *_: (0,) * len(shape))


def _attn_pre_kernel(x_ref, g_ref, w_ref, qn_ref, wuq_ref, kvn_ref, wuk_ref, wuv_ref,
                     cos_ref, sin_ref,
                     qa_ref, ka_ref, va_ref, q4_ref, k4_ref, v4_ref, q16_ref, k16_ref, v16_ref,
                     qm_ref, km_ref, vm_ref, lay):
    xn = _rms(x_ref[...], g_ref[...]).astype(BF16)
    half_a = ROPE_DIM_A // 2
    half_b = MLA_ROPE // 2
    cos_t, sin_t = cos_ref[...], sin_ref[...]
    lane = lax.broadcasted_iota(jnp.int32, cos_t.shape, 1)
    a_lo = lane < ROPE_DIM_A
    a_hi = (lane >= HEAD_DIM_A) & (lane < HEAD_DIM_A + ROPE_DIM_A)
    b_on = (lane >= MLA_NOPE) & (lane < MLA_NOPE + MLA_ROPE)
    ca = jnp.where(a_lo, cos_t, jnp.where(a_hi, pltpu.roll(cos_t, HEAD_DIM_A, 1), 1.0))
    sa = jnp.where(a_lo, sin_t, jnp.where(a_hi, pltpu.roll(sin_t, HEAD_DIM_A, 1), 0.0))
    cb = jnp.where(b_on, pltpu.roll(cos_t, MLA_NOPE - ROPE_DIM_A, 1), 1.0)
    sb = jnp.where(b_on, pltpu.roll(sin_t, MLA_NOPE - ROPE_DIM_A, 1), 0.0)
    first_a = lane % HEAD_DIM_A < half_a
    first_b = (lane >= MLA_NOPE) & (lane < MLA_NOPE + half_b)

    q = _dot(xn, w_ref[:, 0:MIX_A]) * (HEAD_DIM_A ** -0.5 * LOG2E)
    k = _dot(xn, w_ref[:, MIX_A:2 * MIX_A])
    v = _dot(xn, w_ref[:, 2 * MIX_A:3 * MIX_A])
    for c in range(MIX_A // LANES):
        sl = slice(c * LANES, (c + 1) * LANES)
        lay[0, c] = _rope(q[:, sl], ca, sa, half_a, first_a)
        lay[1, c] = _rope(k[:, sl], ca, sa, half_a, first_a)
        lay[2, c] = v[:, sl]
    tm = x_ref.shape[0]
    for a, (nat, by4, by16) in enumerate(((qa_ref, q4_ref, q16_ref), (ka_ref, k4_ref, k16_ref),
                                          (va_ref, v4_ref, v16_ref))):
        for c in range(MIX_A // LANES):
            sl = slice(c * LANES, (c + 1) * LANES)
            nat[:, sl] = lay[a, c].astype(BF16)
            for d, ref in ((4, by4), (16, by16)):
                for r in range(d):
                    ref[0, r, :, sl] = lay[a, c, pl.ds(r, tm // d, stride=d), :].astype(BF16)

    o = 3 * MIX_A
    cq = _dot(xn, w_ref[:, o:o + MLA_Q_RANK])
    ckv = _dot(xn, w_ref[:, o + MLA_Q_RANK:o + MLA_Q_RANK + MLA_KV_RANK])
    kr = _dot(xn, w_ref[:, o + MLA_Q_RANK + MLA_KV_RANK:o + MLA_Q_RANK + MLA_KV_RANK + LANES])
    cqn = _rms(cq, qn_ref[...]).astype(BF16)
    kvn = _rms(ckv, kvn_ref[...]).astype(BF16)
    krr = _rope(kr, cb, sb, half_b, first_b)
    qm = _dot(cqn, wuq_ref[...]) * ((MLA_NOPE + MLA_ROPE) ** -0.5 * LOG2E)
    km = _dot(kvn, wuk_ref[...])
    for h in range(MLA_HEADS):
        sl = slice(h * LANES, (h + 1) * LANES)
        qm_ref[:, sl] = _rope(qm[:, sl], cb, sb, half_b, first_b).astype(BF16)
        km_ref[:, sl] = (km[:, sl] + krr).astype(BF16)
    vm = _dot(kvn, wuv_ref[...])
    low = lax.broadcasted_iota(jnp.int32, vm.shape, 1) % LANES < MLA_V
    vm_ref[...] = jnp.where(low, vm, 1.0).astype(BF16)


def _attn_pre(x2, g, w_cat, qn, wuq, kvn, wuk, wuv, tabs, seq):
    T = x2.shape[0]
    tm = min(ROW_TILE, T)
    nb = seq // tm
    row = lambda n: pl.BlockSpec((tm, n), lambda i: (i, 0))
    wide = MLA_HEADS * LANES
    by_residue = [pl.BlockSpec((1, d, tm // d, MIX_A), lambda i: (i // nb, 0, i % nb, 0))
                  for d in DILATIONS[1:]]
    residue_shapes = [jax.ShapeDtypeStruct((T // seq, d, seq // d, MIX_A), BF16) for d in DILATIONS[1:]]
    return pl.pallas_call(
        _attn_pre_kernel,
        grid=(T // tm,),
        in_specs=[row(D_MODEL), _full((1, D_MODEL)), _full(w_cat.shape), _full((1, MLA_Q_RANK)),
                  _full(wuq.shape), _full((1, MLA_KV_RANK)), _full(wuk.shape), _full(wuv.shape)]
                 + [row(LANES)] * 2,
        out_specs=[row(MIX_A)] * 3 + [by_residue[0]] * 3 + [by_residue[1]] * 3 + [row(wide)] * 3,
        out_shape=[jax.ShapeDtypeStruct((T, MIX_A), BF16)] * 3 + [residue_shapes[0]] * 3
                  + [residue_shapes[1]] * 3 + [jax.ShapeDtypeStruct((T, wide), BF16)] * 3,
        scratch_shapes=[pltpu.VMEM((3, MIX_A // LANES, tm, LANES), F32)],
        compiler_params=_params("parallel"),
        name="attn_pre",
    )(x2, g, w_cat, qn, wuq, kvn, wuk, wuv, *tabs)


def _dilated_kernel(*refs, d, tq, carry, last):
    refs = list(refs)
    q_ref, kc_ref, kp_ref, vc_ref, vp_ref = refs[:5]
    op_ref, lp_ref = refs[5:7] if carry else (None, None)
    kbuf, vbuf = refs[-2:]
    o_ref = refs[7] if carry else refs[5]
    lse_ref = None if last else refs[-3]
    j = pl.program_id(2)
    for r in range(d):
        kbuf[r, 0:SPAN, :] = kp_ref[0, r]
        kbuf[r, SPAN:, :] = kc_ref[0, r]
        vbuf[r, 0:SPAN, :] = vp_ref[0, r]
        vbuf[r, SPAN:, :] = vc_ref[0, r]
    low = lax.broadcasted_iota(jnp.int32, (SPAN, LANES), 1) < HEAD_DIM_A
    qi = lax.broadcasted_iota(jnp.int32, (SPAN, 2 * SPAN), 0)
    ki = lax.broadcasted_iota(jnp.int32, (SPAN, 2 * SPAN), 1)
    band = (ki >= qi) & (ki <= qi + SPAN)
    first_valid = band & ((ki >= SPAN) | (j > 0))
    for r in range(d):
        for sub in range(tq // SPAN):
            valid = first_valid if sub == 0 else band
            q2 = q_ref[0, r, sub * SPAN:(sub + 1) * SPAN, :]
            kk = kbuf[r, sub * SPAN:(sub + 2) * SPAN, :]
            vv = vbuf[r, sub * SPAN:(sub + 2) * SPAN, :]
            os_, ls_ = [], []
            for e in range(2):
                qe = jnp.where(low if e == 0 else jnp.logical_not(low), q2, jnp.zeros_like(q2))
                s = jnp.where(valid, _dot_nt(qe, kk), NEG_INF)
                m = jnp.max(s, axis=-1, keepdims=True)
                pr = jnp.exp2(s - m)
                l = jnp.sum(pr, axis=-1, keepdims=True)
                os_.append(_dot(pr.astype(BF16), vv) / l)
                ls_.append(m + jnp.log(l) * LOG2E)
            o2 = jnp.where(low, os_[0], os_[1])
            l2 = jnp.where(low, ls_[0], ls_[1])
            rows = pl.ds(sub * SPAN, SPAN) if d == 1 else pl.ds(d * sub * SPAN + r, SPAN, stride=d)
            if carry:
                lp = lp_ref[0, 0, rows, :]
                mx = jnp.maximum(lp, l2)
                wp = jnp.exp2(lp - mx)
                wc = jnp.exp2(l2 - mx)
                den = wp + wc
                o2 = (wp * op_ref[0, 0, rows, :] + wc * o2) / den
                l2 = mx + jnp.log(den) * LOG2E
            o_ref[0, 0, rows, :] = o2
            if not last:
                lse_ref[0, 0, rows, :] = l2


def _dilated_branch(q, k, v, d, prev, last):
    B, _, L, _ = q.shape
    pairs = MIX_A // LANES
    rows = min(DIL_ROWS, d * L)
    tq = rows // d
    cur = pl.BlockSpec((1, d, tq, LANES), lambda b, p, j: (b, 0, j, p))
    prv = pl.BlockSpec((1, d, SPAN, LANES), lambda b, p, j: (b, 0, jnp.maximum(j * (tq // SPAN) - 1, 0), p))
    nat = pl.BlockSpec((1, 1, rows, LANES), lambda b, p, j: (b, p, j, 0))
    nat_shape = jax.ShapeDtypeStruct((B, pairs, d * L, LANES), F32)
    carry = prev is not None
    outs = pl.pallas_call(
        functools.partial(_dilated_kernel, d=d, tq=tq, carry=carry, last=last),
        grid=(B, pairs, L // tq),
        in_specs=[cur, cur, prv, cur, prv] + ([nat, nat] if carry else []),
        out_specs=[nat] if last else [nat, nat],
        out_shape=[nat_shape] if last else [nat_shape, nat_shape],
        scratch_shapes=[pltpu.VMEM((d, tq + SPAN, LANES), BF16), pltpu.VMEM((d, tq + SPAN, LANES), BF16)],
        compiler_params=_params("parallel", "parallel", "parallel"),
        name=f"dilated_d{d}",
    )(q, k, k, v, v, *(prev if carry else ()))
    return outs


def _mla_kernel(q_ref, k_ref, v_ref, o_ref, m0, m1, a0, a1, *, tq, tk):
    qi = pl.program_id(2)
    ms, accs = (m0, m1), (a0, a1)
    for e in range(2):
        ms[e][...] = jnp.full(ms[e].shape, NEG_INF, F32)
        accs[e][...] = jnp.zeros(accs[e].shape, F32)

    def step(kv, diag):
        start = pl.multiple_of(kv * tk, tk)
        r0 = 0 if diag is None else diag * tk
        for e in range(2):
            lanes = slice(e * LANES, (e + 1) * LANES)
            s = _dot_nt(q_ref[0, r0:, lanes], k_ref[0, pl.ds(start, tk), lanes])
            if diag is not None:
                row = lax.broadcasted_iota(jnp.int32, s.shape, 0)
                col = lax.broadcasted_iota(jnp.int32, s.shape, 1)
                s = jnp.where(col <= row, s, NEG_INF)
            m_prev = ms[e][r0:, :]
            m_new = jnp.maximum(m_prev, jnp.max(s, axis=-1, keepdims=True))
            alpha = jnp.exp2(m_prev - m_new)
            pr = jnp.concatenate([jnp.exp2(s[:, c * LANES:(c + 1) * LANES] - m_new)
                                  for c in range(tk // LANES)], axis=1).astype(BF16)
            accs[e][r0:, :] = alpha * accs[e][r0:, :] + _dot(pr, v_ref[0, pl.ds(start, tk), lanes])
            ms[e][r0:, :] = m_new

    n_full = qi * (tq // tk)
    n_main = n_full // MLA_UNROLL

    def body(i, c):
        for u in range(MLA_UNROLL):
            step(MLA_UNROLL * i + u, None)
        return c

    def single(i, c):
        step(i, None)
        return c

    lax.fori_loop(0, n_main, body, 0)
    lax.fori_loop(n_main * MLA_UNROLL, n_full, single, 0)

    for t in range(tq // tk):
        step(n_full + t, t)
    low = lax.broadcasted_iota(jnp.int32, (tq, LANES), 1) < MLA_V
    r0 = a0[...] / pltpu.roll(a0[...], MLA_V, 1)
    r1 = a1[...] / pltpu.roll(a1[...], MLA_V, 1)
    o_ref[0] = jnp.where(low, r0, pltpu.roll(r1, MLA_V, 1)).astype(BF16)


def _mla_attention(qm, km, vm):
    B, S, _ = qm.shape
    tq = min(MLA_TQ, S)
    tk = min(MLA_TK, tq)
    pairs = MLA_HEADS // 2
    return pl.pallas_call(
        functools.partial(_mla_kernel, tq=tq, tk=tk),
        grid=(B, pairs, S // tq),
        in_specs=[pl.BlockSpec((1, tq, 2 * LANES), lambda b, p, i: (b, i, p)),
                  pl.BlockSpec((1, S, 2 * LANES), lambda b, p, i: (b, 0, p)),
                  pl.BlockSpec((1, S, 2 * LANES), lambda b, p, i: (b, 0, p))],
        out_specs=pl.BlockSpec((1, tq, LANES), lambda b, p, i: (b, i, p)),
        out_shape=jax.ShapeDtypeStruct((B, S, MIX_B), BF16),
        scratch_shapes=[pltpu.VMEM((tq, LANES), F32)] * 4,
        compiler_params=_params("parallel", "parallel", "arbitrary"),
        name="mla_flash",
    )(qm, km, vm)


def _route_tail(x1, gm_ref, wr_ref, br_ref, x1_ref, h_ref, ri_ref, rg_ref, cnt_ref, cnt_sc):
    x1_ref[...] = x1
    hn = _rms(x1, gm_ref[...])
    h_hi = hn.astype(BF16)
    h_ref[...] = _pack_bf16_pairs(hn)
    h_lo = (hn - h_hi.astype(F32)).astype(BF16)
    both = _dot(h_hi, wr_ref[...])
    logits = both[:, :LANES] + both[:, LANES:] + _dot(h_lo, wr_ref[:, 0:LANES]) + br_ref[...]
    tm = logits.shape[0]
    lane = lax.broadcasted_iota(jnp.int32, (tm, LANES), 1)
    lanef = lane.astype(F32)
    big = float(LANES)
    is_g = (lane >= N_EXPERTS) & (lane < N_EXPERTS + N_GROUPS)
    gl = jnp.where(is_g, logits, NEG_INF)
    gmax = jnp.max(gl, axis=-1, keepdims=True)
    gsum = jnp.sum(jnp.exp(gl - gmax), axis=-1, keepdims=True)
    g_val = 1.0 / gsum
    g_idx = jnp.min(jnp.where(gl == gmax, lanef, big), axis=-1, keepdims=True) - float(N_EXPERTS)
    in_grp = (lane < N_EXPERTS) & ((lane // EXPERTS_PER_GROUP).astype(F32) == g_idx)
    el = jnp.where(in_grp, logits, NEG_INF)
    e1 = jnp.max(el, axis=-1, keepdims=True)
    i1 = jnp.min(jnp.where(el == e1, lanef, big), axis=-1, keepdims=True)
    el2 = jnp.where(lanef == i1, NEG_INF, el)
    e2 = jnp.max(el2, axis=-1, keepdims=True)
    i2 = jnp.min(jnp.where(el2 == e2, lanef, big), axis=-1, keepdims=True)
    t = jnp.exp(e2 - e1)
    p1 = 1.0 / (1.0 + t)
    rg_ref[...] = jnp.where(lane == 0, p1 * g_val, jnp.where(lane == 1, t * p1 * g_val, 0.0))
    @pl.when(pl.program_id(0) == 0)
    def _():
        cnt_sc[...] = jnp.zeros(cnt_sc.shape, F32)

    oh1 = lanef == i1
    oh2 = lanef == i2
    oh = jnp.where(oh1 | oh2, 1.0, 0.0)
    rr = lax.broadcasted_iota(jnp.int32, (tm, tm), 0)
    cc = lax.broadcasted_iota(jnp.int32, (tm, tm), 1)
    strict = jnp.where(cc < rr, 1.0, 0.0).astype(BF16)
    before = _dot(strict, oh.astype(BF16)) + cnt_sc[...]
    rank1 = jnp.sum(jnp.where(oh1, before, 0.0), axis=-1, keepdims=True)
    rank2 = jnp.sum(jnp.where(oh2, before, 0.0), axis=-1, keepdims=True)
    cnt_sc[...] = cnt_sc[...] + jnp.sum(oh, axis=0, keepdims=True)
    cnt_ref[...] = jnp.broadcast_to(cnt_sc[...], cnt_ref.shape).astype(jnp.int32)
    ri_ref[...] = jnp.where(lane == 0, i1, jnp.where(lane == 1, i2, jnp.where(
        lane == 2, rank1, jnp.where(lane == 3, rank2, 0.0)))).astype(jnp.int32)


def _tail_specs(tm):
    row = lambda n: pl.BlockSpec((tm, n), lambda i: (i, 0))
    in_specs = [_full((1, D_MODEL)), _full((D_MODEL, 2 * LANES)), _full((1, LANES))]
    out_specs = [row(D_MODEL), row(D_MODEL // 2), row(LANES), row(LANES), _full((8, LANES))]
    return in_specs, out_specs


def _tail_shapes(T):
    return [jax.ShapeDtypeStruct((T, D_MODEL), F32), jax.ShapeDtypeStruct((T, D_MODEL // 2), jnp.uint32),
            jax.ShapeDtypeStruct((T, LANES), jnp.int32), jax.ShapeDtypeStruct((T, LANES), F32),
            jax.ShapeDtypeStruct((8, LANES), jnp.int32)]


def _attn_post_kernel(oa_ref, ob_ref, x_ref, wo_ref, *tail_refs):
    oa = jnp.concatenate([oa_ref[0, p] for p in range(MIX_A // LANES)], axis=1).astype(BF16)
    y = _dot(oa, wo_ref[0:MIX_A, :]) + _dot(ob_ref[...], wo_ref[MIX_A:, :])
    _route_tail(x_ref[...] + y, *tail_refs)


def _attn_post(oa, ob, x2, wo, tail):
    T = x2.shape[0]
    tm = min(ROW_TILE, T)
    nb = oa.shape[2] // tm
    row = lambda n: pl.BlockSpec((tm, n), lambda i: (i, 0))
    t_in, t_out = _tail_specs(tm)
    return pl.pallas_call(
        _attn_post_kernel,
        grid=(T // tm,),
        in_specs=[pl.BlockSpec((1, MIX_A // LANES, tm, LANES), lambda i: (i // nb, 0, i % nb, 0)),
                  row(MIX_B), row(D_MODEL), _full(wo.shape)] + t_in,
        out_specs=t_out,
        out_shape=_tail_shapes(T),
        scratch_shapes=[pltpu.VMEM((1, LANES), F32)],
        compiler_params=_params("arbitrary"),
        name="attn_post",
    )(oa, ob, x2, wo, *tail)


def _expert_kernel(be_ref, nr_ref, xs_ref, wg_ref, wu_ref, wd_ref, y_ref, wg_bf, wu_bf, wd_bf):
    i = pl.program_id(0)

    @pl.when((i == 0) | (be_ref[i] != be_ref[jnp.maximum(i - 1, 0)]))
    def _():
        wg_bf[...] = wg_ref[0].astype(BF16)
        wu_bf[...] = wu_ref[0].astype(BF16)
        wd_bf[...] = wd_ref[0].astype(BF16)

    @pl.when(nr_ref[i] > 0)
    def _():
        live = lax.broadcasted_iota(jnp.int32, xs_ref.shape, 0) < nr_ref[i]
        xb = _unpack_bf16_pairs(jnp.where(live, xs_ref[...], jnp.uint32(0))).astype(BF16)
        g = _dot(xb, wg_bf[...])
        u = _dot(xb, wu_bf[...])
        hid = (g * _sigmoid(g) * u).astype(BF16)
        y_ref[...] = _pack_bf16_pairs(_dot(hid, wd_bf[...]))

    @pl.when(nr_ref[i] == 0)
    def _():
        y_ref[...] = jnp.zeros(y_ref.shape, y_ref.dtype)


def _expert_blocks(xs, block_expert, block_rows, layer, wg, wu, wd):
    rows = xs.shape[0]
    bm = MOE_ROWS
    half = D_MODEL // 2
    grid_spec = pltpu.PrefetchScalarGridSpec(
        num_scalar_prefetch=2,
        grid=(rows // bm,),
        in_specs=[pl.BlockSpec((bm, half), lambda i, be, nr: (i, 0)),
                  pl.BlockSpec((None, 1, D_MODEL, D_EXPERT), lambda i, be, nr: (layer, be[i], 0, 0)),
                  pl.BlockSpec((None, 1, D_MODEL, D_EXPERT), lambda i, be, nr: (layer, be[i], 0, 0)),
                  pl.BlockSpec((None, 1, D_EXPERT, D_MODEL), lambda i, be, nr: (layer, be[i], 0, 0))],
        out_specs=pl.BlockSpec((bm, half), lambda i, be, nr: (i, 0)),
        scratch_shapes=[pltpu.VMEM((D_MODEL, D_EXPERT), BF16), pltpu.VMEM((D_MODEL, D_EXPERT), BF16),
                        pltpu.VMEM((D_EXPERT, D_MODEL), BF16)],
    )
    return pl.pallas_call(
        _expert_kernel,
        grid_spec=grid_spec,
        out_shape=jax.ShapeDtypeStruct((rows, half), jnp.uint32),
        compiler_params=_params("arbitrary"),
        name="moe_experts",
    )(block_expert, block_rows, xs, wg, wu, wd)


def _moe_sum(x_ref, r0_ref, r1_ref, rg_ref):
    rg = rg_ref[...]
    return (x_ref[...] + rg[:, 0:1] * _unpack_bf16_pairs(r0_ref[...])
            + rg[:, 1:2] * _unpack_bf16_pairs(r1_ref[...]))


def _combine_kernel(x_ref, r0_ref, r1_ref, rg_ref, gf_ref, o_ref):
    o_ref[...] = _rms(_moe_sum(x_ref, r0_ref, r1_ref, rg_ref), gf_ref[...])


def _moe_combine(x1, yg, rg, gfinal):
    T = x1.shape[0]
    tm = min(ROW_TILE, T)
    half = D_MODEL // 2
    row = lambda n: pl.BlockSpec((tm, n), lambda i: (i, 0))
    return pl.pallas_call(
        _combine_kernel,
        grid=(T // tm,),
        in_specs=[row(D_MODEL), row(half), pl.BlockSpec((tm, half), lambda i: (i + T // tm, 0)),
                  row(LANES), _full((1, D_MODEL))],
        out_specs=row(D_MODEL),
        out_shape=jax.ShapeDtypeStruct((T, D_MODEL), F32),
        compiler_params=_params("parallel"),
        name="moe_combine",
    )(x1, yg, yg, rg, gfinal)


def _sc_rows(n_rows):
    per_worker = n_rows // SC_WORKERS
    assert n_rows % SC_WORKERS == 0 and per_worker % SC_CHUNK == 0
    return per_worker, per_worker // SC_CHUNK


def _sc_dispatch(hp, dest0, dest1, rows):
    T, W = hp.shape
    per_w, n_sub = _sc_rows(T)
    mesh = plsc.VectorSubcoreMesh(core_axis_name="c", subcore_axis_name="s")

    @functools.partial(
        pl.kernel, mesh=mesh,
        out_type=jax.ShapeDtypeStruct((rows, W), hp.dtype),
        scratch_types=[pltpu.VMEM((n_sub, SC_CHUNK), jnp.int32), pltpu.VMEM((n_sub, SC_CHUNK), jnp.int32),
                       pltpu.VMEM((SC_CHUNK, W), hp.dtype)],
    )
    def k(hp_hbm, d0_hbm, d1_hbm, out_hbm, d0_v, d1_v, buf):
        wid = lax.axis_index("s") * SC_CORES + lax.axis_index("c")
        pltpu.sync_copy(d0_hbm.at[wid], d0_v)
        pltpu.sync_copy(d1_hbm.at[wid], d1_v)

        @pl.loop(0, n_sub)
        def _(j):
            pltpu.sync_copy(hp_hbm.at[pl.ds(wid * per_w + j * SC_CHUNK, SC_CHUNK)], buf)
            pltpu.sync_copy(buf, out_hbm.at[d0_v.at[j]])
            pltpu.sync_copy(buf, out_hbm.at[d1_v.at[j]])

    shp = (SC_WORKERS, n_sub, SC_CHUNK)
    return k(hp, dest0.reshape(shp), dest1.reshape(shp))


def _sc_gather(table, idx):
    M = idx.shape[0]
    W = table.shape[1]
    per_w, n_sub = _sc_rows(M)
    assert n_sub % 2 == 0
    mesh = plsc.VectorSubcoreMesh(core_axis_name="c", subcore_axis_name="s")

    @functools.partial(
        pl.kernel, mesh=mesh,
        out_type=jax.ShapeDtypeStruct((M, W), table.dtype),
        scratch_types=[pltpu.VMEM((n_sub, SC_CHUNK), jnp.int32),
                       pltpu.VMEM((SC_CHUNK, W), table.dtype), pltpu.VMEM((SC_CHUNK, W), table.dtype),
                       pltpu.SemaphoreType.DMA, pltpu.SemaphoreType.DMA],
    )
    def k(table_hbm, idx_hbm, out_hbm, idx_v, buf0, buf1, sem0, sem1):
        wid = lax.axis_index("s") * SC_CORES + lax.axis_index("c")
        pltpu.sync_copy(idx_hbm.at[wid], idx_v)
        bufs, sems = (buf0, buf1), (sem0, sem1)

        def gather(j, slot):
            return pltpu.make_async_copy(table_hbm.at[idx_v.at[j]], bufs[slot], sems[slot])

        gather(0, 0).start()

        @pl.loop(0, n_sub // 2)
        def _(jj):
            for slot in range(2):
                j = jj * 2 + slot

                @pl.when(j + 1 < n_sub)
                def _():
                    gather(j + 1, 1 - slot).start()

                gather(j, slot).wait()
                pltpu.sync_copy(bufs[slot], out_hbm.at[pl.ds(wid * per_w + j * SC_CHUNK, SC_CHUNK)])

    return k(table, idx.reshape(SC_WORKERS, n_sub, SC_CHUNK))


def _moe_rows(hp, ri, cnt, layer, wg, wu, wd):
    T = hp.shape[0]
    bm = MOE_ROWS
    experts = jnp.arange(N_EXPERTS, dtype=jnp.int32)
    counts = cnt[0, :N_EXPERTS]
    padded = (counts + bm - 1) // bm * bm
    pad_ends = jnp.cumsum(padded)
    pad_starts = pad_ends - padded
    dest2 = ri[:, 2:4] + jnp.sum(jnp.where(ri[:, 0:2, None] > experts, padded, 0), axis=-1)
    n_blocks = (2 * T) // bm + N_EXPERTS
    block_start = jnp.arange(n_blocks, dtype=jnp.int32) * bm
    block_expert = jnp.minimum(jnp.sum((pad_ends[None, :] <= block_start[:, None]).astype(jnp.int32), axis=1),
                               N_EXPERTS - 1)
    block_rows = jnp.clip(counts[block_expert] - (block_start - pad_starts[block_expert]), 0, bm)
    xs = _sc_dispatch(hp, dest2[:, 0], dest2[:, 1], n_blocks * bm)
    y_rows = _expert_blocks(xs, block_expert, block_rows, layer, wg, wu, wd)
    return _sc_gather(y_rows, jnp.concatenate([dest2[:, 0], dest2[:, 1]]))


def _ssd_pre_kernel(x_ref, r0_ref, r1_ref, rg_ref, g_ref, w_ref, x2_ref, z_ref, xbc_ref, dt_ref):
    x2 = _moe_sum(x_ref, r0_ref, r1_ref, rg_ref)
    x2_ref[...] = x2
    xn = _rms(x2, g_ref[...]).astype(BF16)
    step = 512
    for c in range(0, D_INNER, step):
        z_ref[:, c:c + step] = _dot(xn, w_ref[:, c:c + step]).astype(BF16)
    for c in range(0, XBC_DIM, step):
        xbc_ref[:, c:c + step] = _dot(xn, w_ref[:, D_INNER + c:D_INNER + c + step]).astype(BF16)
    dt_ref[...] = _dot(xn, w_ref[:, D_INNER + XBC_DIM:])


def _ssd_pre(x1, yg, rg, g, w_cat):
    T = x1.shape[0]
    tm = min(ROW_TILE, T)
    half = D_MODEL // 2
    row = lambda n: pl.BlockSpec((tm, n), lambda i: (i, 0))
    return pl.pallas_call(
        _ssd_pre_kernel,
        grid=(T // tm,),
        in_specs=[row(D_MODEL), row(half), pl.BlockSpec((tm, half), lambda i: (i + T // tm, 0)), row(LANES),
                  _full((1, D_MODEL)), _full(w_cat.shape)],
        out_specs=[row(D_MODEL), row(D_INNER), row(XBC_DIM), row(LANES)],
        out_shape=[jax.ShapeDtypeStruct((T, D_MODEL), F32), jax.ShapeDtypeStruct((T, D_INNER), BF16),
                   jax.ShapeDtypeStruct((T, XBC_DIM), BF16), jax.ShapeDtypeStruct((T, LANES), F32)],
        compiler_params=_params("parallel"),
        name="ssd_pre",
    )(x1, yg, yg, rg, g, w_cat)


def _ssd_scan_kernel(xbc_ref, dt_ref, cw_ref, cb_ref, dtb_ref, alog_ref, dsk_ref, y_ref,
                     ubuf, abuf, state):
    Q = CHUNK
    H = Q // 2
    c = pl.program_id(1)
    n_blk = XBC_DIM // LANES

    @pl.when(c == 0)
    def _():
        ubuf[:, 0:8, :] = jnp.zeros((n_blk, 8, LANES), F32)
        state[...] = jnp.zeros(state.shape, F32)

    phases = 8
    for blk in range(n_blk):
        lanes = slice(blk * LANES, (blk + 1) * LANES)
        ubuf[blk, 8:8 + Q, :] = xbc_ref[0, :, lanes].astype(F32)
        taps = [cw_ref[k:k + 1, lanes] for k in range(D_CONV)]
        bias = cb_ref[:, lanes]
        for s in range(phases):
            acc = bias
            for k in range(D_CONV):
                acc = acc + taps[k] * ubuf[blk, pl.ds(8 - (D_CONV - 1) + k + s, Q // phases, stride=phases), :]
            abuf[blk, pl.ds(s, Q // phases, stride=phases), :] = acc * _sigmoid(acc)
        ubuf[blk, 0:8, :] = ubuf[blk, Q:Q + 8, :]

    x_dt = dt_ref[0] + dtb_ref[...]
    dt = jnp.maximum(x_dt, 0.0) + jnp.log(1.0 + jnp.exp(-jnp.abs(x_dt)))
    da = dt * (-jnp.exp(alog_ref[...]) * LOG2E)
    ri = lax.broadcasted_iota(jnp.int32, (Q, Q), 0)
    ci = lax.broadcasted_iota(jnp.int32, (Q, Q), 1)
    tril = jnp.where(ci <= ri, 1.0, 0.0).astype(BF16)
    d_hi = da.astype(BF16)
    r1 = da - d_hi.astype(F32)
    d_mid = r1.astype(BF16)
    d_lo = (r1 - d_mid.astype(F32)).astype(BF16)
    cs = _dot(tril, d_hi) + _dot(tril, d_mid) + _dot(tril, d_lo)
    cs_last = cs[Q - 1:Q, :]
    cs_t = jnp.transpose(cs)
    dt_t = jnp.transpose(dt)
    lw = jnp.log(dt) * LOG2E + (cs_last - cs)
    dec_last = jnp.exp2(cs_last)
    low = lax.broadcasted_iota(jnp.int32, (Q, LANES), 1) < SSD_HEADDIM
    low1 = low[0:1, :]
    tri = lax.broadcasted_iota(jnp.int32, (H, H), 1) <= lax.broadcasted_iota(jnp.int32, (H, H), 0)
    hg = SSD_HEADS // SSD_GROUPS
    x_blocks = D_INNER // LANES
    for g in range(SSD_GROUPS):
        b_g = abuf[x_blocks + g]
        c_g = abuf[x_blocks + SSD_GROUPS + g].astype(BF16)
        cb = _dot_nt(c_g, b_g.astype(BF16))
        b_t = jnp.transpose(b_g).astype(BF16)
        st = state[g]
        y_inter = _dot(c_g, st.astype(BF16))
        xw_parts, dl_parts = [], []
        for p in range(hg // 2):
            j = g * (hg // 2) + p
            lanes = slice(j * LANES, (j + 1) * LANES)
            x_p = abuf[j]
            x16 = x_p.astype(BF16)
            ys, ecs, wend = [], [], []
            for e in range(2):
                h = 2 * j + e
                col = jnp.broadcast_to(cs[:, h:h + 1], (Q, LANES))
                row = cs_t[h:h + 1, :]
                dtr = dt_t[h:h + 1, :]
                tl = jnp.exp2(jnp.where(tri, col[:H] - row[:, :H], NEG_INF)) * (cb[:H, :H] * dtr[:, :H])
                bl = jnp.exp2(col[H:] - row[:, :H]) * (cb[H:, :H] * dtr[:, :H])
                br = jnp.exp2(jnp.where(tri, col[H:] - row[:, H:], NEG_INF)) * (cb[H:, H:] * dtr[:, H:])
                y_top = _dot(tl.astype(BF16), x16[:H])
                y_bot = _dot(jnp.concatenate([bl, br], axis=1).astype(BF16), x16)
                ys.append(jnp.concatenate([y_top, y_bot], axis=0))
                ecs.append(jnp.exp2(col))
                wend.append(jnp.exp2(jnp.broadcast_to(lw[:, h:h + 1], (Q, LANES))))
            y_p = (jnp.where(low, ys[0], ys[1])
                   + y_inter[:, p * LANES:(p + 1) * LANES] * jnp.where(low, ecs[0], ecs[1])
                   + dsk_ref[:, lanes] * x_p)
            y_ref[0, :, lanes] = y_p.astype(y_ref.dtype)
            xw_parts.append((x_p * jnp.where(low, wend[0], wend[1])).astype(BF16))
            dl_parts.append(jnp.where(low1, dec_last[:, 2 * j:2 * j + 1], dec_last[:, 2 * j + 1:2 * j + 2]))
        xw = jnp.concatenate(xw_parts, axis=1)
        dl = jnp.concatenate(dl_parts, axis=1)
        state[g] = st * dl + _dot(b_t, xw)


def _ssd_scan(xbc, dt, conv_w, conv_b, dt_bias, a_log, d_skip):
    B, S, _ = xbc.shape
    Q = CHUNK
    hg = SSD_HEADS // SSD_GROUPS
    return pl.pallas_call(
        _ssd_scan_kernel,
        grid=(B, S // Q),
        in_specs=[pl.BlockSpec((1, Q, XBC_DIM), lambda b, c: (b, c, 0)),
                  pl.BlockSpec((1, Q, LANES), lambda b, c: (b, c, 0)),
                  _full((D_CONV, XBC_DIM)), _full((1, XBC_DIM)), _full((1, LANES)), _full((1, LANES)),
                  _full((1, D_INNER))],
        out_specs=pl.BlockSpec((1, Q, D_INNER), lambda b, c: (b, c, 0)),
        out_shape=jax.ShapeDtypeStruct((B, S, D_INNER), BF16),
        scratch_shapes=[pltpu.VMEM((XBC_DIM // LANES, Q + 8, LANES), F32),
                        pltpu.VMEM((XBC_DIM // LANES, Q, LANES), F32),
                        pltpu.VMEM((SSD_GROUPS, D_STATE, hg * SSD_HEADDIM), F32)],
        compiler_params=_params("parallel", "arbitrary"),
        name="ssd_scan",
    )(xbc, dt, conv_w, conv_b, dt_bias, a_log, d_skip)


def _ssd_post_kernel(y_ref, z_ref, gn_ref, x_ref, wo_ref, *tail_refs):
    gsz = D_INNER // SSD_GROUPS
    out = None
    for g in range(SSD_GROUPS):
        sl = slice(g * gsz, (g + 1) * gsz)
        z = z_ref[:, sl].astype(F32)
        yz = y_ref[:, sl].astype(F32) * (z * _sigmoid(z))
        part = _dot(_rms(yz, gn_ref[:, sl]).astype(BF16), wo_ref[sl, :])
        out = part if out is None else out + part
    _route_tail(x_ref[...] + out, *tail_refs)


def _ssd_post(y, z, gn, x2, wo, tail):
    T = x2.shape[0]
    tm = min(ROW_TILE, T)
    row = lambda n: pl.BlockSpec((tm, n), lambda i: (i, 0))
    t_in, t_out = _tail_specs(tm)
    return pl.pallas_call(
        _ssd_post_kernel,
        grid=(T // tm,),
        in_specs=[row(D_INNER), row(D_INNER), _full((1, D_INNER)), row(D_MODEL), _full(wo.shape)] + t_in,
        out_specs=t_out,
        out_shape=_tail_shapes(T),
        scratch_shapes=[pltpu.VMEM((1, LANES), F32)],
        compiler_params=_params("arbitrary"),
        name="ssd_post",
    )(y, z, gn, x2, wo, *tail)


def _rope_tables(positions):
    pos = positions.reshape(-1, 1).astype(F32)
    lane = jnp.arange(LANES)
    freq = jnp.zeros((LANES,), F32)
    sign = jnp.zeros((LANES,), F32)
    for rot, lead in ((ROPE_DIM_A, 0), (MLA_ROPE, ROPE_DIM_A)):
        half = rot // 2
        off = lane - lead
        rotary = (off >= 0) & (off < rot)
        inv = ROPE_THETA ** (-(2 * (off % half)).astype(F32) / rot)
        freq = jnp.where(rotary, inv, freq)
        sign = jnp.where(rotary, jnp.where(off < half, -1.0, 1.0), sign)
    ang = pos * freq[None, :]
    return [jnp.cos(ang), jnp.sin(ang) * sign[None, :]]


def _pad_cols(w, n):
    return jnp.pad(w, ((0, 0), (0, n - w.shape[1])))


def _head_pad(w, width):
    K = w.shape[0]
    w = w.reshape(K, -1, width)
    return jnp.pad(w, ((0, 0), (0, 0), (0, LANES - width))).reshape(K, -1)


def _router_weights(w_group, b_group, w_router, b_router):
    w = _pad_cols(jnp.concatenate([w_router, w_group], axis=1), LANES)
    b = _pad_cols(jnp.concatenate([b_router, b_group])[None, :], LANES)
    hi = w.astype(BF16)
    lo = (w - hi.astype(F32)).astype(BF16)
    return jnp.concatenate([hi, lo], axis=1), b


def kernel(x, positions, attn_norm, w_in_attn, mla_q_norm, w_uq, mla_kv_norm, w_ukv, w_out_attn,
           ssd_norm, w_in_ssd, conv_w, conv_b, dt_bias, a_log, d_skip, gate_norm, w_out_ssd,
           moe_norm, w_group, b_group, w_router, b_router, w_gate, w_up, w_down, final_norm):
    B, S, D = x.shape
    T = B * S
    x2 = x.reshape(T, D)
    tabs = _rope_tables(positions)

    def tail_params(layer):
        wr, b = _router_weights(w_group[layer], b_group[layer], w_router[layer], b_router[layer])
        return [moe_norm[layer][None, :], wr, b]

    w_in = w_in_attn[0]
    kr_off = 3 * MIX_A + MLA_Q_RANK + MLA_KV_RANK
    kr_cols = jnp.pad(w_in[:, kr_off:], ((0, 0), (MLA_NOPE, LANES - MLA_NOPE - MLA_ROPE)))
    w_cat = jnp.concatenate([w_in[:, :kr_off], kr_cols], axis=1).astype(BF16)
    wuq = _head_pad(w_uq[0], MLA_NOPE + MLA_ROPE).astype(BF16)
    ukv = w_ukv[0].reshape(MLA_KV_RANK, MLA_HEADS, MLA_NOPE + MLA_V)
    wuk = _head_pad(ukv[:, :, :MLA_NOPE].reshape(MLA_KV_RANK, -1), MLA_NOPE).astype(BF16)
    wuv = _head_pad(ukv[:, :, MLA_NOPE:].reshape(MLA_KV_RANK, -1), MLA_V).astype(BF16)
    pre = _attn_pre(x2, attn_norm[0][None, :], w_cat, mla_q_norm[0][None, :], wuq,
                    mla_kv_norm[0][None, :], wuk, wuv, tabs, S)
    qm, km, vm = pre[9:]
    shp = lambda t: t.reshape(B, S, t.shape[-1])
    qkv = [[t.reshape(B, 1, S, MIX_A) for t in pre[0:3]], pre[3:6], pre[6:9]]
    prev = None
    for n, d in enumerate(DILATIONS):
        prev = _dilated_branch(*qkv[n], d, prev, n == len(DILATIONS) - 1)
    ob = _mla_attention(shp(qm), shp(km), shp(vm)).reshape(T, MIX_B)
    x1, h, ri, rg, cnt = _attn_post(prev[0], ob, x2, w_out_attn[0].astype(BF16), tail_params(0))
    yg = _moe_rows(h, ri, cnt, 0, w_gate, w_up, w_down)

    w_cat = _pad_cols(w_in_ssd[0], D_INNER + XBC_DIM + LANES).astype(BF16)
    x2, z, xbc, dt = _ssd_pre(x1, yg, rg, ssd_norm[0][None, :], w_cat)
    y = _ssd_scan(xbc.reshape(B, S, XBC_DIM), dt.reshape(B, S, LANES), conv_w[0], conv_b[0][None, :],
                  _pad_cols(dt_bias[0][None, :], LANES), _pad_cols(a_log[0][None, :], LANES),
                  jnp.repeat(d_skip[0], SSD_HEADDIM)[None, :])
    x1, h, ri, rg, cnt = _ssd_post(y.reshape(T, D_INNER), z, gate_norm[0][None, :], x2,
                                   w_out_ssd[0].astype(BF16), tail_params(1))
    yg = _moe_rows(h, ri, cnt, 1, w_gate, w_up, w_down)
    out = _moe_combine(x1, yg, rg, final_norm[None, :])
    return out.reshape(B, S, D)
```
